```python
import math
import jax, jax.numpy as jnp
from jax import lax
import numpy as np


D_MODEL = 1024
BATCH = 8
SEQ = 4096
DEPTH = 4

F32 = jnp.float32
GRID_W = 64
HEAD_DIM = 64
BRANCH_WIDTH = 256
N_BRANCHES = 4
LN_EPS = 1e-5
RMS_EPS = 1e-6

HY_WIDTH = 256
HY_ORDER = 2
HY_POS_EMB = 33
HY_FILT_HID = 64
HY_TARGET = 1e-2
HY_SHORT_DECAY_PCT = 0.3
HY_LONG_DECAY_PCT = 1.5

DIL_GROUPS = ((128, 1), (512, 4), (2048, 16))
DIL_N_GROUPS = 3
DIL_HEADS_PER_GROUP = 4
DIL_N_HEADS = DIL_N_GROUPS * DIL_HEADS_PER_GROUP
DIL_WIDTH = DIL_N_HEADS * HEAD_DIM
DIL_BLOCK = 128

S5_WIDTH = 256
S5_GROUP = 16
S5_N_GROUPS = S5_WIDTH // S5_GROUP
S5_STATE = 64
S5_DT_MIN = 1e-3
S5_DT_MAX = 1e-1

GQA_HEADS = 4
GQA_KV_HEADS = 2
GQA_WIDTH = GQA_HEADS * HEAD_DIM
GQA_KV_WIDTH = GQA_KV_HEADS * HEAD_DIM
GQA_BLOCK = 128
ROPE_THETA = 10000.0

IN_SPLITS = (3 * HY_WIDTH, 3 * DIL_WIDTH, S5_WIDTH, GQA_WIDTH, GQA_KV_WIDTH, GQA_KV_WIDTH, N_BRANCHES * D_MODEL)
N_IN = sum(IN_SPLITS)

N_EXPERTS = 16
EC_CAPACITY = 2
EXPERT_FF = 1024

DN_ALPHA = (2 * DEPTH) ** 0.25
DN_BETA = (8 * DEPTH) ** -0.25

kernel_name = 'hybrid_parallel_gated_encoder'


def layer_norm(x, g, b):
    xf = x.astype(F32)
    mu = jnp.mean(xf, axis=-1, keepdims=True)
    var = jnp.mean(jnp.square(xf - mu), axis=-1, keepdims=True)
    return ((xf - mu) * lax.rsqrt(var + LN_EPS) * g + b).astype(x.dtype)


def rms_norm_heads(x, g):
    xf = x.astype(F32)
    return (xf * lax.rsqrt(jnp.mean(xf * xf, axis=-1, keepdims=True) + RMS_EPS) * g).astype(x.dtype)


def short_conv_centred(u, w, b):
    s = u.shape[1]
    up = jnp.pad(u, ((0, 0), (1, 1), (0, 0)))
    return up[:, :s] * w[0] + up[:, 1:s + 1] * w[1] + up[:, 2:] * w[2] + b


def hyena_pos_features(seq_len):
    t = jnp.linspace(0.0, 1.0, seq_len, dtype=F32)[:, None]
    bands = (HY_POS_EMB - 1) // 2
    f = jnp.linspace(1e-4, bands - 1, bands, dtype=F32)[None, :]
    w = 2.0 * math.pi * jnp.arange(seq_len, dtype=F32)[:, None] / seq_len
    z = jnp.concatenate([t, jnp.cos(f * w), -jnp.sin(f * w)], axis=-1)
    return t, z


def hyena_filters(t, z, w1, b1, w2, b2, w3, freq):
    seq_len = z.shape[0]
    f = jnp.sin(freq[0] * (z @ w1 + b1))
    f = jnp.sin(freq[1] * (f @ w2 + b2))
    f = (f @ w3).astype(F32).reshape(seq_len, 2, HY_ORDER, HY_WIDTH)
    max_decay = math.log(HY_TARGET) / HY_SHORT_DECAY_PCT
    min_decay = math.log(HY_TARGET) / HY_LONG_DECAY_PCT
    deltas = jnp.abs(jnp.linspace(min_decay, max_decay, HY_WIDTH, dtype=F32))
    f = f * jnp.exp(-t * deltas)[:, None, None, :]
    fwd, bwd = f[:, 0], f[:, 1]
    k = jnp.concatenate([fwd, jnp.zeros_like(fwd[:1]), bwd[:0:-1]], axis=0)
    return k * lax.rsqrt(jnp.sum(k * k, axis=0, keepdims=True))


def fft_long_conv(u, filt_freq, bias):
    s = u.shape[1]
    uf = u.astype(F32)
    y = jnp.fft.irfft(jnp.fft.rfft(uf, n=2 * s, axis=1) * filt_freq, n=2 * s, axis=1)[:, :s]
    return (y + uf * bias).astype(u.dtype)


def hyena_mixer(proj, conv_w, conv_b, filt, bias):
    u = short_conv_centred(proj, conv_w, conv_b)
    parts = jnp.split(u, HY_ORDER + 1, axis=-1)
    z = parts[0]
    filt_freq = jnp.fft.rfft(filt, axis=0)
    for n in range(HY_ORDER):
        z = parts[n + 1] * fft_long_conv(z, filt_freq[:, n], bias[n])
    return z


def dilated_attention(q, k, v):
    bsz, s = q.shape[0], q.shape[1]
    offs = np.stack([np.arange(-(w // 2), w // 2 + 1, d) for (w, d) in DIL_GROUPS])
    offs = np.repeat(offs, DIL_HEADS_PER_GROUP, axis=0)
    slopes = 2.0 ** (-8.0 * np.arange(1, DIL_N_HEADS + 1) / DIL_N_HEADS)
    off = jnp.asarray(offs, dtype=jnp.int32)
    alibi = jnp.asarray(-slopes[:, None] * np.abs(offs), dtype=F32)
    hidx = jnp.arange(DIL_N_HEADS)[:, None, None]
    scale = HEAD_DIM ** -0.5
    q = q.transpose(0, 2, 1, 3)
    k = k.transpose(0, 2, 1, 3)
    v = v.transpose(0, 2, 1, 3)
    nblk = s // DIL_BLOCK
    qb = q.reshape(bsz, DIL_N_HEADS, nblk, DIL_BLOCK, HEAD_DIM).transpose(2, 0, 1, 3, 4)
    starts = jnp.arange(nblk, dtype=jnp.int32) * DIL_BLOCK

    def one_block(args):
        qblk, start = args
        pos = start + jnp.arange(DIL_BLOCK, dtype=jnp.int32)[None, :, None] + off[:, None, :]
        valid = (pos >= 0) & (pos < s)
        pos = jnp.clip(pos, 0, s - 1)
        kg = k[:, hidx, pos]
        vg = v[:, hidx, pos]
        sc = jnp.einsum('bhqd,bhqkd->bhqk', qblk, kg).astype(F32) * scale + alibi[:, None, :]
        sc = jnp.where(valid[None], sc, -jnp.inf)
        lse = jax.nn.logsumexp(sc, axis=-1)
        p = jnp.exp(sc - lse[..., None]).astype(vg.dtype)
        o = jnp.einsum('bhqk,bhqkd->bhqd', p, vg)
        o = o.reshape(bsz, DIL_N_GROUPS, DIL_HEADS_PER_GROUP, DIL_BLOCK, HEAD_DIM)
        wgt = jax.nn.softmax(lse.reshape(bsz, DIL_N_GROUPS, DIL_HEADS_PER_GROUP, DIL_BLOCK), axis=1)
        return jnp.einsum('bghq,bghqd->bhqd', wgt.astype(o.dtype), o)

    out = lax.map(one_block, (qb, starts))
    return out.transpose(1, 0, 3, 2, 4).reshape(bsz, s, DIL_HEADS_PER_GROUP * HEAD_DIM)


def _ssm_combine(e1, e2):
    a1, b1 = e1
    a2, b2 = e2
    return a1 * a2, a2 * b1 + b2


def s5_mixer(u, lam_re, lam_im, log_dt, b_re, b_im, c_re, c_im, d, glu_w, glu_b):
    bsz, s = u.shape[0], u.shape[1]
    uf = u.astype(F32)
    ug = uf.reshape(bsz, s, S5_N_GROUPS, S5_GROUP).astype(jnp.complex64)
    y = d.astype(F32) * uf
    for direction in range(2):
        lam = lax.complex(lam_re[direction].astype(F32), lam_im[direction].astype(F32))
        dt = jnp.exp(log_dt[direction].astype(F32))[:, None]
        a_bar = jnp.exp(lam * dt)
        bmat = lax.complex(b_re[direction].astype(F32), b_im[direction].astype(F32))
        b_bar = ((a_bar - 1.0) / lam)[..., None] * bmat
        bu = jnp.einsum('bsgc,gpc->bsgp', ug, b_bar)
        a = jnp.broadcast_to(a_bar, bu.shape)
        _, xs = lax.associative_scan(_ssm_combine, (a, bu), axis=1, reverse=(direction == 1))
        cmat = lax.complex(c_re[direction].astype(F32), c_im[direction].astype(F32))
        y = y + jnp.real(jnp.einsum('bsgp,gcp->bsgc', xs, cmat)).reshape(bsz, s, S5_WIDTH)
    z = jax.nn.gelu(y)
    return (z * jax.nn.sigmoid(z @ glu_w + glu_b)).astype(u.dtype)


def axial_rope_angles(s):
    n_rows = s // GRID_W
    rows = jnp.repeat(jnp.arange(n_rows, dtype=F32), GRID_W)
    cols = (jnp.arange(s) % GRID_W).astype(F32)
    half = HEAD_DIM // 2
    inv = ROPE_THETA ** (-jnp.arange(0, half, 2, dtype=F32) / half)
    return rows[:, None] * inv, cols[:, None] * inv


def _rope_half(x, ang):
    c = jnp.cos(ang)[:, None, :]
    sn = jnp.sin(ang)[:, None, :]
    x1, x2 = jnp.split(x, 2, axis=-1)
    return jnp.concatenate([x1 * c - x2 * sn, x2 * c + x1 * sn], axis=-1)


def apply_axial_rope(x, ang_r, ang_c):
    xf = x.astype(F32)
    half = HEAD_DIM // 2
    return jnp.concatenate([_rope_half(xf[..., :half], ang_r), _rope_half(xf[..., half:], ang_c)], axis=-1).astype(x.dtype)


def gqa_attention(q, k, v, q_norm, k_norm, ang_r, ang_c):
    bsz, s = q.shape[0], q.shape[1]
    rep = GQA_HEADS // GQA_KV_HEADS
    q = apply_axial_rope(rms_norm_heads(q, q_norm), ang_r, ang_c)
    k = apply_axial_rope(rms_norm_heads(k, k_norm), ang_r, ang_c)
    nblk = s // GQA_BLOCK
    qb = q.reshape(bsz, nblk, GQA_BLOCK, GQA_KV_HEADS, rep, HEAD_DIM).transpose(1, 0, 3, 4, 2, 5)
    kt = k.transpose(0, 2, 1, 3)
    vt = v.transpose(0, 2, 1, 3)
    scale = HEAD_DIM ** -0.5

    def one_block(qblk):
        sc = jnp.einsum('bgrqd,bgkd->bgrqk', qblk, kt).astype(F32) * scale
        p = jax.nn.softmax(sc, axis=-1).astype(vt.dtype)
        return jnp.einsum('bgrqk,bgkd->bgrqd', p, vt)

    o = lax.map(one_block, qb)
    return o.transpose(1, 0, 4, 2, 3, 5).reshape(bsz, s, GQA_WIDTH)


def mixing_sublayer(h, w_in, hy_conv_w, hy_conv_b, hy_w1, hy_b1, hy_w2, hy_b2, hy_w3, hy_freq, hy_bias,
                    s5_lam_re, s5_lam_im, s5_log_dt, s5_b_re, s5_b_im, s5_c_re, s5_c_im, s5_d, s5_glu_w, s5_glu_b,
                    gqa_q_norm, gqa_k_norm, w_branch, w_out, pos_t, pos_z, ang_r, ang_c):
    bsz, s = h.shape[0], h.shape[1]
    proj = h @ w_in
    split_points = np.cumsum(IN_SPLITS)[:-1].tolist()
    hy_in, dil_qkv, s5_in, gq, gk, gv, gate_logits = jnp.split(proj, split_points, axis=-1)
    filt = hyena_filters(pos_t, pos_z, hy_w1, hy_b1, hy_w2, hy_b2, hy_w3, hy_freq)
    y_a = hyena_mixer(hy_in, hy_conv_w, hy_conv_b, filt, hy_bias)
    dq, dk, dv = [t.reshape(bsz, s, DIL_N_HEADS, HEAD_DIM) for t in jnp.split(dil_qkv, 3, axis=-1)]
    y_b = dilated_attention(dq, dk, dv)
    y_c = s5_mixer(s5_in, s5_lam_re, s5_lam_im, s5_log_dt, s5_b_re, s5_b_im, s5_c_re, s5_c_im, s5_d, s5_glu_w, s5_glu_b)
    y_d = gqa_attention(gq.reshape(bsz, s, GQA_HEADS, HEAD_DIM), gk.reshape(bsz, s, GQA_KV_HEADS, HEAD_DIM),
                        gv.reshape(bsz, s, GQA_KV_HEADS, HEAD_DIM), gqa_q_norm, gqa_k_norm, ang_r, ang_c)
    gate_logits = gate_logits.reshape(bsz, s, N_BRANCHES, D_MODEL)
    merged = None
    for i, y in enumerate((y_a, y_b, y_c, y_d)):
        term = jax.nn.sigmoid(gate_logits[:, :, i]) * (y @ w_branch[i])
        merged = term if merged is None else merged + term
    return merged @ w_out


def expert_choice_moe(h, router_w, w_gate, w_up, w_down):
    bsz, s = h.shape[0], h.shape[1]
    cap = (EC_CAPACITY * s) // N_EXPERTS
    aff = jax.nn.softmax((h @ router_w).astype(F32), axis=-1)
    g, idx = lax.top_k(aff.transpose(0, 2, 1), cap)
    bidx = jnp.arange(bsz)[:, None, None]
    xs = h[bidx, idx]
    hid = jax.nn.silu(jnp.einsum('becd,edf->becf', xs, w_gate)) * jnp.einsum('becd,edf->becf', xs, w_up)
    out = jnp.einsum('becf,efd->becd', hid, w_down) * g[..., None].astype(h.dtype)
    return jnp.zeros_like(h).at[bidx, idx].add(out)


def setup_inputs(seed: int = 0) -> dict:
    key = jax.random.key(seed)
    ks = iter(jax.random.split(key, 40))

    def nrm(shape, scale):
        return scale * jax.random.normal(next(ks), shape, F32)

    L = DEPTH
    G, P, CG = S5_N_GROUPS, S5_STATE, S5_GROUP
    lam_im0 = math.pi * jnp.arange(P, dtype=F32)
    return {
        'x': nrm((BATCH, SEQ, D_MODEL), 1.0),
        'ln_in_g': 1.0 + nrm((D_MODEL,), 0.02),
        'ln_in_b': nrm((D_MODEL,), 0.02),
        'w_in': nrm((L, D_MODEL, N_IN), D_MODEL ** -0.5),
        'hy_conv_w': nrm((L, 3, 3 * HY_WIDTH), 3 ** -0.5),
        'hy_conv_b': nrm((L, 3 * HY_WIDTH), 0.02),
        'hy_w1': nrm((L, HY_POS_EMB, HY_FILT_HID), HY_POS_EMB ** -0.5),
        'hy_b1': nrm((L, HY_FILT_HID), 0.1),
        'hy_w2': nrm((L, HY_FILT_HID, HY_FILT_HID), HY_FILT_HID ** -0.5),
        'hy_b2': nrm((L, HY_FILT_HID), 0.1),
        'hy_w3': nrm((L, HY_FILT_HID, 2 * HY_ORDER * HY_WIDTH), HY_FILT_HID ** -0.5),
        'hy_freq': 1.0 + nrm((L, 2, HY_FILT_HID), 0.02),
        'hy_bias': nrm((L, HY_ORDER, HY_WIDTH), 0.5),
        's5_lam_re': -0.5 + nrm((L, 2, G, P), 0.01),
        's5_lam_im': lam_im0 + nrm((L, 2, G, P), 0.01),
        's5_log_dt': jax.random.uniform(next(ks), (L, 2, G), F32, math.log(S5_DT_MIN), math.log(S5_DT_MAX)),
        's5_b_re': nrm((L, 2, G, P, CG), (2 * CG) ** -0.5),
        's5_b_im': nrm((L, 2, G, P, CG), (2 * CG) ** -0.5),
        's5_c_re': nrm((L, 2, G, CG, P), 0.5),
        's5_c_im': nrm((L, 2, G, CG, P), 0.5),
        's5_d': nrm((L, S5_WIDTH), 1.0),
        's5_glu_w': nrm((L, S5_WIDTH, S5_WIDTH), S5_WIDTH ** -0.5),
        's5_glu_b': nrm((L, S5_WIDTH), 0.02),
        'gqa_q_norm': 1.0 + nrm((L, HEAD_DIM), 0.02),
        'gqa_k_norm': 1.0 + nrm((L, HEAD_DIM), 0.02),
        'w_branch': nrm((L, N_BRANCHES, BRANCH_WIDTH, D_MODEL), BRANCH_WIDTH ** -0.5),
        'w_out': nrm((L, D_MODEL, D_MODEL), DN_BETA * D_MODEL ** -0.5),
        'ln1_g': 1.0 + nrm((L, D_MODEL), 0.02),
        'ln1_b': nrm((L, D_MODEL), 0.02),
        'router_w': nrm((L, D_MODEL, N_EXPERTS), D_MODEL ** -0.5),
        'exp_w_gate': nrm((L, N_EXPERTS, D_MODEL, EXPERT_FF), D_MODEL ** -0.5),
        'exp_w_up': nrm((L, N_EXPERTS, D_MODEL, EXPERT_FF), D_MODEL ** -0.5),
        'exp_w_down': nrm((L, N_EXPERTS, EXPERT_FF, D_MODEL), DN_BETA * EXPERT_FF ** -0.5),
        'ln2_g': 1.0 + nrm((L, D_MODEL), 0.02),
        'ln2_b': nrm((L, D_MODEL), 0.02),
    }


def reference(x, ln_in_g, ln_in_b, w_in, hy_conv_w, hy_conv_b, hy_w1, hy_b1, hy_w2, hy_b2, hy_w3, hy_freq, hy_bias,
              s5_lam_re, s5_lam_im, s5_log_dt, s5_b_re, s5_b_im, s5_c_re, s5_c_im, s5_d, s5_glu_w, s5_glu_b,
              gqa_q_norm, gqa_k_norm, w_branch, w_out, ln1_g, ln1_b,
              router_w, exp_w_gate, exp_w_up, exp_w_down, ln2_g, ln2_b):
    s = x.shape[1]
    pos_t, pos_z = hyena_pos_features(s)
    ang_r, ang_c = axial_rope_angles(s)
    h = layer_norm(x, ln_in_g, ln_in_b)
    for l in range(DEPTH):
        mix = mixing_sublayer(h, w_in[l], hy_conv_w[l], hy_conv_b[l], hy_w1[l], hy_b1[l], hy_w2[l], hy_b2[l],
                              hy_w3[l], hy_freq[l], hy_bias[l],
                              s5_lam_re[l], s5_lam_im[l], s5_log_dt[l], s5_b_re[l], s5_b_im[l], s5_c_re[l],
                              s5_c_im[l], s5_d[l], s5_glu_w[l], s5_glu_b[l],
                              gqa_q_norm[l], gqa_k_norm[l], w_branch[l], w_out[l], pos_t, pos_z, ang_r, ang_c)
        h = layer_norm(DN_ALPHA * h + mix, ln1_g[l], ln1_b[l])
        ffn = expert_choice_moe(h, router_w[l], exp_w_gate[l], exp_w_up[l], exp_w_down[l])
        h = layer_norm(DN_ALPHA * h + ffn, ln2_g[l], ln2_b[l])
    return h
```

```python
import functools
import math

import jax
import jax.numpy as jnp
import numpy as np
from jax import lax
from jax.experimental import pallas as pl
from jax.experimental.pallas import tpu as pltpu

F32 = jnp.float32
BF16 = jnp.bfloat16
I32 = jnp.int32
U32 = jnp.uint32

D_MODEL = 1024
DEPTH = 4
HEAD_DIM = 64
GRID_W = 64
N_BRANCHES = 4
LN_EPS = 1e-5
RMS_EPS = 1e-6

HY_WIDTH = 256
HY_ORDER = 2
HY_POS_EMB = 33
HY_TARGET = 1e-2
HY_SHORT_DECAY_PCT = 0.3
HY_LONG_DECAY_PCT = 1.5

DIL_GROUPS = ((128, 1), (512, 4), (2048, 16))
DIL_HEADS_PER_GROUP = 4
DIL_N_HEADS = 12
DIL_WIDTH = DIL_N_HEADS * HEAD_DIM

S5_WIDTH = 256
S5_GROUP = 16
S5_N_GROUPS = 16
S5_STATE = 64

GQA_HEADS = 4
GQA_KV_HEADS = 2
GQA_WIDTH = GQA_HEADS * HEAD_DIM
GQA_KV_WIDTH = GQA_KV_HEADS * HEAD_DIM
ROPE_THETA = 10000.0

N_EXPERTS = 16
EC_CAPACITY = 2
EXPERT_FF = 1024

DN_ALPHA = (2 * DEPTH) ** 0.25

COL_HY = 0
COL_DIL = 3 * HY_WIDTH
COL_S5 = COL_DIL + 3 * DIL_WIDTH
COL_GQ = COL_S5 + S5_WIDTH
COL_GK = COL_GQ + GQA_WIDTH
COL_GV = COL_GK + GQA_KV_WIDTH
N_MIX = COL_GV + GQA_KV_WIDTH
N_IN = N_MIX + N_BRANCHES * D_MODEL

LANE = 128
SUBLANE = 8
VMEM_LIMIT = 56 * 1024 * 1024

HY_CB = 8
HY_BLK = 256
S5_CHUNK = 64
TOPK_JB = 512


def _cparams(sem, vmem=None):
    return pltpu.CompilerParams(dimension_semantics=sem, vmem_limit_bytes=vmem)


def _layer_norm(x, g, b):
    mu = jnp.mean(x, axis=-1, keepdims=True)
    xc = x - mu
    var = jnp.mean(xc * xc, axis=-1, keepdims=True)
    return xc * lax.rsqrt(var + LN_EPS) * g + b


def _ln_kernel(x_ref, g_ref, b_ref, o_ref):
    o_ref[...] = _layer_norm(x_ref[...], g_ref[...], b_ref[...])


def ln_pallas(x, g, b, tm=1024):
    n, d = x.shape
    return pl.pallas_call(
        _ln_kernel,
        grid=(n // tm,),
        in_specs=[pl.BlockSpec((tm, d), lambda i: (i, 0)),
                  pl.BlockSpec((1, d), lambda i: (0, 0)),
                  pl.BlockSpec((1, d), lambda i: (0, 0))],
        out_specs=pl.BlockSpec((tm, d), lambda i: (i, 0)),
        out_shape=jax.ShapeDtypeStruct((n, d), F32),
        compiler_params=_cparams(("parallel",)),
        name="ln_in",
    )(x, g.reshape(1, d), b.reshape(1, d))


def _res_ln_kernel(h_ref, f_ref, g_ref, b_ref, o_ref):
    o_ref[...] = _layer_norm(DN_ALPHA * h_ref[...] + f_ref[...], g_ref[...], b_ref[...])


def res_ln_pallas(h, f, g, b, tm=1024):
    n, d = h.shape
    return pl.pallas_call(
        _res_ln_kernel,
        grid=(n // tm,),
        in_specs=[pl.BlockSpec((tm, d), lambda i: (i, 0)),
                  pl.BlockSpec((tm, d), lambda i: (i, 0)),
                  pl.BlockSpec((1, d), lambda i: (0, 0)),
                  pl.BlockSpec((1, d), lambda i: (0, 0))],
        out_specs=pl.BlockSpec((tm, d), lambda i: (i, 0)),
        out_shape=jax.ShapeDtypeStruct((n, d), F32),
        compiler_params=_cparams(("parallel",)),
        name="res_ln",
    )(h, f, g.reshape(1, d), b.reshape(1, d))


def _inproj_kernel(h_ref, w_ref, o_ref):
    o_ref[...] = jnp.dot(h_ref[...].astype(BF16), w_ref[...], preferred_element_type=F32)


def inproj_pallas(h, w_bf16, tm=1024, tn=1280):
    n, d = h.shape
    nout = w_bf16.shape[1]
    return pl.pallas_call(
        _inproj_kernel,
        grid=(nout // tn, n // tm),
        in_specs=[pl.BlockSpec((tm, d), lambda j, i: (i, 0)),
                  pl.BlockSpec((d, tn), lambda j, i: (0, j))],
        out_specs=pl.BlockSpec((tm, tn), lambda j, i: (i, j)),
        out_shape=jax.ShapeDtypeStruct((n, nout), F32),
        compiler_params=_cparams(("parallel", "parallel"), VMEM_LIMIT),
        name="in_proj",
    )(h, w_bf16)


def _hy_kernel(cw_ref, cb_ref, hb_ref,
               v_ref, x1_ref, x2_ref, k_ref,
               o_ref,
               f2_ref, ua_ref, ub_ref,
               *, seq, nb):
    g = pl.program_id(0)
    nblk = seq // HY_BLK
    npad = nblk - 1
    nlb = 2 * seq // LANE
    rows = nb * nblk

    lane_i = lax.broadcasted_iota(I32, (LANE, LANE), 1)
    row_i = lax.broadcasted_iota(I32, (LANE, LANE), 0)
    keep_cur = lane_i >= row_i
    t_i = lax.broadcasted_iota(I32, (nb, seq), 1)

    def short_conv(x, ch):
        w0 = cw_ref[ch]
        w1 = cw_ref[3 * HY_WIDTH + ch]
        w2 = cw_ref[6 * HY_WIDTH + ch]
        xm = jnp.where(t_i == 0, 0.0, pltpu.roll(x, 1, 1))
        xp = jnp.where(t_i == seq - 1, 0.0, pltpu.roll(x, seq - 1, 1))
        return xm * w0 + x * w1 + xp * w2 + cb_ref[ch]

    def long_conv(u, order, ci):
        krow = k_ref[order, pl.ds(ci, 1), :]
        prev = None
        for n in range(nlb):
            blk = jnp.broadcast_to(krow[:, n * LANE:(n + 1) * LANE], (LANE, LANE))
            cur = pltpu.roll(blk, 0, 1, stride=1, stride_axis=0)
            if n >= 1:
                out = jnp.where(keep_cur, cur, prev).astype(BF16)
                f2_ref[0:LANE, n * LANE:(n + 1) * LANE] = out
                if n + 1 < nlb:
                    f2_ref[LANE:2 * LANE, (n + 1) * LANE:(n + 2) * LANE] = out
            prev = cur
        blocks = [u[:, j * HY_BLK:(j + 1) * HY_BLK] for j in range(nblk)]
        za = jnp.zeros((npad * nb, HY_BLK), F32)
        ua = jnp.concatenate([za] + blocks + [za, jnp.zeros((2 * nb, HY_BLK), F32)], axis=0)
        zb = jnp.zeros(((npad - 1) * nb, HY_BLK), F32)
        ub = jnp.concatenate([zb] + blocks + [za, jnp.zeros((3 * nb, HY_BLK), F32)], axis=0)
        ua_ref[...] = ua.astype(BF16)
        ub_ref[...] = ub.astype(BF16)
        acc = jnp.zeros((rows, HY_BLK), F32)
        for d in range(-npad, npad + 1):
            blk0 = npad - d
            if blk0 % 2 == 0:
                lhs = ua_ref[blk0 * nb:blk0 * nb + rows, :]
            else:
                lhs = ub_ref[(blk0 - 1) * nb:(blk0 - 1) * nb + rows, :]
            c0 = seq + HY_BLK * d
            acc = acc + jnp.dot(lhs, f2_ref[:, c0:c0 + HY_BLK], preferred_element_type=F32)
        return jnp.concatenate([acc[j * nb:(j + 1) * nb, :] for j in range(nblk)], axis=1)

    def body(ci, carry):
        ch = g * HY_CB + ci
        v = short_conv(v_ref[ci], ch)
        x1 = short_conv(x1_ref[ci], HY_WIDTH + ch)
        x2 = short_conv(x2_ref[ci], 2 * HY_WIDTH + ch)
        z = x1 * (long_conv(v, 0, ci) + v * hb_ref[ch])
        z = x2 * (long_conv(z, 1, ci) + z * hb_ref[HY_WIDTH + ch])
        o_ref[ci] = z
        return carry

    lax.fori_loop(0, HY_CB, body, 0)


def hyena_pallas(u_t, kext, conv_w, conv_b, hy_bias):
    _, nb, seq = u_t.shape
    assert nb == SUBLANE and seq % HY_BLK == 0
    nblk = seq // HY_BLK
    ncb = HY_WIDTH // HY_CB
    pad_rows = (2 * (nblk - 1) + nblk + 2) * nb
    grid_spec = pltpu.PrefetchScalarGridSpec(
        num_scalar_prefetch=3,
        grid=(ncb,),
        in_specs=[
            pl.BlockSpec((HY_CB, nb, seq), lambda g, *_: (g, 0, 0)),
            pl.BlockSpec((HY_CB, nb, seq), lambda g, *_: (ncb + g, 0, 0)),
            pl.BlockSpec((HY_CB, nb, seq), lambda g, *_: (2 * ncb + g, 0, 0)),
            pl.BlockSpec((HY_ORDER, HY_CB, 2 * seq), lambda g, *_: (0, g, 0)),
        ],
        out_specs=pl.BlockSpec((HY_CB, nb, seq), lambda g, *_: (g, 0, 0)),
        scratch_shapes=[
            pltpu.VMEM((HY_BLK, 2 * seq), BF16),
            pltpu.VMEM((pad_rows, HY_BLK), BF16),
            pltpu.VMEM((pad_rows, HY_BLK), BF16),
        ],
    )
    return pl.pallas_call(
        functools.partial(_hy_kernel, seq=seq, nb=nb),
        grid_spec=grid_spec,
        out_shape=jax.ShapeDtypeStruct((HY_WIDTH, nb, seq), F32),
        compiler_params=_cparams(("arbitrary",)),
        name="hyena_conv",
    )(conv_w.reshape(-1), conv_b, hy_bias.reshape(-1), u_t, u_t, u_t, kext)


def _hyena_filters_ext(seq, w1, b1, w2, b2, w3, freq):
    hp = lax.Precision.HIGHEST
    t = jnp.linspace(0.0, 1.0, seq, dtype=F32)[:, None]
    bands = (HY_POS_EMB - 1) // 2
    f = jnp.linspace(1e-4, bands - 1, bands, dtype=F32)[None, :]
    w = 2.0 * math.pi * jnp.arange(seq, dtype=F32)[:, None] / seq
    z = jnp.concatenate([t, jnp.cos(f * w), -jnp.sin(f * w)], axis=-1)
    h = jnp.sin(freq[0] * (jnp.dot(z, w1, precision=hp) + b1))
    h = jnp.sin(freq[1] * (jnp.dot(h, w2, precision=hp) + b2))
    h = jnp.dot(h, w3, precision=hp).reshape(seq, 2, HY_ORDER, HY_WIDTH)
    max_decay = math.log(HY_TARGET) / HY_SHORT_DECAY_PCT
    min_decay = math.log(HY_TARGET) / HY_LONG_DECAY_PCT
    deltas = jnp.abs(jnp.linspace(min_decay, max_decay, HY_WIDTH, dtype=F32))
    h = h * jnp.exp(-t * deltas)[:, None, None, :]
    fwd, bwd = h[:, 0], h[:, 1]
    k = jnp.concatenate([fwd, jnp.zeros_like(fwd[:1]), bwd[:0:-1]], axis=0)
    k = k * lax.rsqrt(jnp.sum(k * k, axis=0, keepdims=True))
    return jnp.roll(k, seq, axis=0).transpose(1, 2, 0)


DIL_QB = 128
DIL_KB = 256
DIL_HALF = 64


def _dil_kernel(slope_ref, q_ref, k_ref, v_ref, o_ref, oacc_ref, lse_ref, *, seq):
    pj = pl.program_id(1)
    g = pl.program_id(2)
    head1 = lax.broadcasted_iota(I32, (DIL_QB, LANE), 1) >= HEAD_DIM
    qi = lax.broadcasted_iota(I32, (DIL_QB, DIL_KB), 0)
    ki = lax.broadcasted_iota(I32, (DIL_QB, DIL_KB), 1)
    scale = HEAD_DIM ** -0.5

    for gi, (w, d) in enumerate(DIL_GROUPS):
        assert w // (2 * d) == DIL_HALF
        ln = seq // d

        @pl.when(g == gi)
        def _(gi=gi, d=d, ln=ln):
            def task(t, carry):
                r = t % d
                mq = (t // d) * DIL_QB
                mk = jnp.clip(mq - DIL_HALF, 0, ln - DIL_KB)
                q = q_ref[0, pl.ds(r + d * mq, DIL_QB, stride=d), :]
                k = k_ref[0, pl.ds(r + d * mk, DIL_KB, stride=d), :].astype(BF16)
                v = v_ref[0, pl.ds(r + d * mk, DIL_KB, stride=d), :].astype(BF16)
                rel = jnp.abs((ki + mk) - (qi + mq))
                valid = rel <= DIL_HALF
                dist = (rel * d).astype(F32)
                outs, lses = [], []
                for hh in range(2):
                    slope = slope_ref[4 * gi + 2 * pj + hh]
                    qm = jnp.where(head1 == (hh == 1), q, 0.0).astype(BF16)
                    s = lax.dot_general(qm, k, (((1,), (1,)), ((), ())),
                                        preferred_element_type=F32)
                    s = s * scale - slope * dist
                    s = jnp.where(valid, s, -jnp.inf)
                    m = jnp.max(s, axis=1, keepdims=True)
                    p = jnp.exp(s - m)
                    l = jnp.sum(p, axis=1, keepdims=True)
                    o = jnp.dot(p.astype(BF16), v, preferred_element_type=F32)
                    outs.append(o / l)
                    lses.append(m + jnp.log(l))
                o = jnp.where(head1, outs[1], outs[0])
                lse = jnp.where(head1, lses[1], lses[0])
                oacc_ref[gi, pl.ds(r + d * mq, DIL_QB, stride=d), :] = o
                lse_ref[gi, pl.ds(r + d * mq, DIL_QB, stride=d), :] = lse
                return carry

            lax.fori_loop(0, seq // DIL_QB, task, 0)

    @pl.when(g == len(DIL_GROUPS) - 1)
    def _():
        l0, l1, l2 = lse_ref[0], lse_ref[1], lse_ref[2]
        m = jnp.maximum(jnp.maximum(l0, l1), l2)
        e0, e1, e2 = jnp.exp(l0 - m), jnp.exp(l1 - m), jnp.exp(l2 - m)
        num = e0 * oacc_ref[0] + e1 * oacc_ref[1] + e2 * oacc_ref[2]
        o_ref[0] = num / (e0 + e1 + e2)


def dilated_pallas(proj3, slopes):
    nb, seq, _ = proj3.shape
    ng = len(DIL_GROUPS)
    qb, kb, vb = COL_DIL // LANE, (COL_DIL + DIL_WIDTH) // LANE, (COL_DIL + 2 * DIL_WIDTH) // LANE
    grid_spec = pltpu.PrefetchScalarGridSpec(
        num_scalar_prefetch=1,
        grid=(nb, 2, ng),
        in_specs=[
            pl.BlockSpec((1, seq, LANE), lambda b, p, g, *_: (b, 0, qb + 2 * g + p)),
            pl.BlockSpec((1, seq, LANE), lambda b, p, g, *_: (b, 0, kb + 2 * g + p)),
            pl.BlockSpec((1, seq, LANE), lambda b, p, g, *_: (b, 0, vb + 2 * g + p)),
        ],
        out_specs=pl.BlockSpec((1, seq, LANE), lambda b, p, g, *_: (b, 0, p)),
        scratch_shapes=[pltpu.VMEM((ng, seq, LANE), F32), pltpu.VMEM((ng, seq, LANE), F32)],
    )
    return pl.pallas_call(
        functools.partial(_dil_kernel, seq=seq),
        grid_spec=grid_spec,
        out_shape=jax.ShapeDtypeStruct((nb, seq, 2 * LANE), F32),
        compiler_params=_cparams(("parallel", "parallel", "arbitrary"), VMEM_LIMIT),
        name="dilated_attn",
    )(slopes, proj3, proj3, proj3)


def _s5_kernel(u_ref, m_ref, e_ref, f_ref, a_ref, o_ref, v_scr, x_scr, *, nb, nchunk):
    u = u_ref[0].astype(BF16)
    v_scr[...] = jnp.dot(u, e_ref[0], preferred_element_type=F32)
    a_re = jnp.broadcast_to(a_ref[0, 0:1, :], (nb, LANE))
    a_im = jnp.broadcast_to(a_ref[0, 1:2, :], (nb, LANE))
    fwd = lax.broadcasted_iota(I32, (nb, LANE), 1) < S5_STATE
    x_re = jnp.zeros((nb, LANE), F32)
    x_im = jnp.zeros((nb, LANE), F32)
    hs = S5_STATE
    for k in range(nchunk):
        kb = nchunk - 1 - k
        x_scr[k * nb:(k + 1) * nb, 0:hs] = x_re[:, 0:hs]
        x_scr[k * nb:(k + 1) * nb, LANE:LANE + hs] = x_im[:, 0:hs]
        x_scr[kb * nb:(kb + 1) * nb, hs:LANE] = x_re[:, hs:LANE]
        x_scr[kb * nb:(kb + 1) * nb, LANE + hs:2 * LANE] = x_im[:, hs:LANE]
        vf = v_scr[k * nb:(k + 1) * nb, :]
        vb = v_scr[kb * nb:(kb + 1) * nb, :]
        v_re = jnp.where(fwd, vf[:, 0:LANE], vb[:, 0:LANE])
        v_im = jnp.where(fwd, vf[:, LANE:2 * LANE], vb[:, LANE:2 * LANE])
        n_re = a_re * x_re - a_im * x_im + v_re
        n_im = a_re * x_im + a_im * x_re + v_im
        x_re, x_im = n_re, n_im
    y = jnp.dot(u, m_ref[0], preferred_element_type=F32)
    y = y + jnp.dot(x_scr[...].astype(BF16), f_ref[0], preferred_element_type=F32)
    o_ref[0] = y


def s5_pallas(u_g, m_mat, e_mat, f_mat, a_vec, nb):
    ng, rows, width = u_g.shape
    nchunk = rows // nb
    ns = 4 * S5_STATE
    return pl.pallas_call(
        functools.partial(_s5_kernel, nb=nb, nchunk=nchunk),
        grid=(ng,),
        in_specs=[
            pl.BlockSpec((1, rows, width), lambda g: (g, 0, 0)),
            pl.BlockSpec((1, width, width), lambda g: (g, 0, 0)),
            pl.BlockSpec((1, width, ns), lambda g: (g, 0, 0)),
            pl.BlockSpec((1, ns, width), lambda g: (g, 0, 0)),
            pl.BlockSpec((1, 2, LANE), lambda g: (g, 0, 0)),
        ],
        out_specs=pl.BlockSpec((1, rows, width), lambda g: (g, 0, 0)),
        out_shape=jax.ShapeDtypeStruct((ng, rows, width), F32),
        scratch_shapes=[pltpu.VMEM((rows, ns), F32), pltpu.VMEM((rows, ns), F32)],
        compiler_params=_cparams(("parallel",), VMEM_LIMIT),
        name="s5_scan",
    )(u_g, m_mat, e_mat, f_mat, a_vec)


def _cmul(ar, ai, br, bi):
    return ar * br - ai * bi, ar * bi + ai * br


def _s5_matrices(lam_re, lam_im, log_dt, b_re, b_im, c_re, c_im):
    t = S5_CHUNK
    hp = lax.Precision.HIGHEST
    dt = jnp.exp(log_dt)[..., None]
    zr, zi = lam_re * dt, lam_im * dt
    mag = jnp.exp(zr)
    ar, ai = mag * jnp.cos(zi), mag * jnp.sin(zi)
    den = lam_re * lam_re + lam_im * lam_im
    qr = ((ar - 1.0) * lam_re + ai * lam_im) / den
    qi = (ai * lam_re - (ar - 1.0) * lam_im) / den
    bbr, bbi = _cmul(qr[..., None], qi[..., None], b_re, b_im)
    n = jnp.arange(t + 1, dtype=F32)[:, None, None, None]
    pmag = jnp.exp(n * zr[None])
    pr, pi = pmag * jnp.cos(n * zi[None]), pmag * jnp.sin(n * zi[None])

    wr, wi = _cmul(c_re[None], c_im[None], pr[:t, :, :, None, :], pi[:t, :, :, None, :])
    kk = (jnp.einsum('tdgop,dgpc->dtgoc', wr, bbr, precision=hp)
          - jnp.einsum('tdgop,dgpc->dtgoc', wi, bbi, precision=hp))
    ti = jnp.arange(t)
    lag = ti[None, :] - ti[:, None]
    kf = kk[0][jnp.clip(lag, 0, t - 1)]
    kb = kk[1][jnp.clip(-lag, 0, t - 1)]
    kern = (jnp.where((lag >= 0)[:, :, None, None, None], kf, 0.0)
            + jnp.where((lag <= 0)[:, :, None, None, None], kb, 0.0))
    m_mat = kern.transpose(2, 0, 4, 1, 3).reshape(S5_N_GROUPS, t * S5_GROUP, t * S5_GROUP)

    ef_r, ef_i = _cmul(pr[:t][::-1, 0][..., None], pi[:t][::-1, 0][..., None], bbr[0][None], bbi[0][None])
    eb_r, eb_i = _cmul(pr[:t, 1][..., None], pi[:t, 1][..., None], bbr[1][None], bbi[1][None])
    e_mat = jnp.concatenate([ef_r, eb_r, ef_i, eb_i], axis=2)
    e_mat = e_mat.transpose(1, 0, 3, 2).reshape(S5_N_GROUPS, t * S5_GROUP, 4 * S5_STATE)

    ff_r, ff_i = _cmul(c_re[0][None], c_im[0][None], pr[1:, 0][:, :, None, :], pi[1:, 0][:, :, None, :])
    fb_r, fb_i = _cmul(c_re[1][None], c_im[1][None],
                       pr[1:][::-1, 1][:, :, None, :], pi[1:][::-1, 1][:, :, None, :])
    f_mat = jnp.concatenate([ff_r, fb_r, -ff_i, -fb_i], axis=3)
    f_mat = f_mat.transpose(1, 3, 0, 2).reshape(S5_N_GROUPS, 4 * S5_STATE, t * S5_GROUP)

    a_vec = jnp.stack([jnp.concatenate([pr[t, 0], pr[t, 1]], axis=-1),
                       jnp.concatenate([pi[t, 0], pi[t, 1]], axis=-1)], axis=1)
    return m_mat.astype(BF16), e_mat.astype(BF16), f_mat.astype(BF16), a_vec


def _gqa_prep_kernel(q_ref, k_ref, v_ref, cos_ref, sin_ref, qn_ref, kn_ref, ones_ref, qo_ref, ko_ref, vo_ref):
    lane = lax.broadcasted_iota(I32, cos_ref.shape, 1)
    first16 = (lane & 31) < 16
    low = lane < HEAD_DIM
    cos = cos_ref[...]
    sin = sin_ref[...]
    ones = ones_ref[...]

    def rms(x, gain):
        sq = x * x
        hi = sq.astype(BF16)
        lo = (sq - hi.astype(F32)).astype(BF16)
        ms = (jnp.dot(hi, ones, preferred_element_type=F32)
              + jnp.dot(lo, ones, preferred_element_type=F32)) * (1.0 / HEAD_DIM)
        return x * lax.rsqrt(ms + RMS_EPS) * gain

    def rope(x):
        swapped = jnp.where(first16, pltpu.roll(x, LANE - 16, 1), pltpu.roll(x, 16, 1))
        return x * cos + swapped * sin

    def dup(x):
        sw = pltpu.roll(x, HEAD_DIM, 1)
        return jnp.concatenate([jnp.where(low, x, sw), jnp.where(low, sw, x)], axis=1)

    scale = HEAD_DIM ** -0.5
    q = q_ref[...]
    qn = qn_ref[...]
    qo_ref[...] = jnp.concatenate(
        [rope(rms(q[:, :LANE], qn)) * scale, rope(rms(q[:, LANE:], qn)) * scale], axis=1).astype(BF16)
    ko_ref[...] = dup(rope(rms(k_ref[...], kn_ref[...]))).astype(BF16)
    vo_ref[...] = dup(v_ref[...]).astype(BF16)


def gqa_prep_pallas(proj, cos_t, sin_t, qn, kn, ones_bd, seq, tm=1024):
    n = proj.shape[0]
    nsb = seq // tm
    w = GQA_WIDTH
    sd = jax.ShapeDtypeStruct
    return pl.pallas_call(
        _gqa_prep_kernel,
        grid=(n // tm,),
        in_specs=[
            pl.BlockSpec((tm, w), lambda i: (i, COL_GQ // w)),
            pl.BlockSpec((tm, LANE), lambda i: (i, COL_GK // LANE)),
            pl.BlockSpec((tm, LANE), lambda i: (i, COL_GV // LANE)),
            pl.BlockSpec((tm, LANE), lambda i: (i % nsb, 0)),
            pl.BlockSpec((tm, LANE), lambda i: (i % nsb, 0)),
            pl.BlockSpec((1, LANE), lambda i: (0, 0)),
            pl.BlockSpec((1, LANE), lambda i: (0, 0)),
            pl.BlockSpec((LANE, LANE), lambda i: (0, 0)),
        ],
        out_specs=[pl.BlockSpec((tm, w), lambda i: (i, 0))] * 3,
        out_shape=[sd((n, w), BF16)] * 3,
        compiler_params=_cparams(("parallel",)),
        name="gqa_prep",
    )(proj, proj, proj, cos_t, sin_t, qn, kn, ones_bd)


GQA_TQ = 256
GQA_KC = 1024


def _gqa_kernel(q_ref, k_ref, v_ref, o_ref, *, seq):
    q = q_ref[0]
    head1 = lax.broadcasted_iota(I32, q.shape, 1) >= HEAD_DIM
    outs = []
    for hh in range(2):
        qm = jnp.where(head1 == (hh == 1), q, jnp.zeros_like(q))
        m = jnp.full((GQA_TQ, 1), -jnp.inf, F32)
        l = jnp.zeros((GQA_TQ, 1), F32)
        acc = jnp.zeros((GQA_TQ, LANE), F32)
        for c in range(seq // GQA_KC):
            k = k_ref[0, c * GQA_KC:(c + 1) * GQA_KC, :]
            v = v_ref[0, c * GQA_KC:(c + 1) * GQA_KC, :]
            s = lax.dot_general(qm, k, (((1,), (1,)), ((), ())), preferred_element_type=F32)
            mn = jnp.maximum(m, jnp.max(s, axis=1, keepdims=True))
            alpha = jnp.exp(m - mn)
            p = jnp.exp(s - mn)
            l = alpha * l + jnp.sum(p, axis=1, keepdims=True)
            acc = alpha * acc + jnp.dot(p.astype(BF16), v, preferred_element_type=F32)
            m = mn
        outs.append(acc / l)
    o_ref[0] = jnp.where(head1, outs[1], outs[0])


def gqa_pallas(qr, kd, vd, nb, seq):
    q3, k3, v3 = (a.reshape(nb, seq, GQA_WIDTH) for a in (qr, kd, vd))
    return pl.pallas_call(
        functools.partial(_gqa_kernel, seq=seq),
        grid=(nb, GQA_KV_HEADS, seq // GQA_TQ),
        in_specs=[
            pl.BlockSpec((1, GQA_TQ, LANE), lambda b, g, i: (b, i, g)),
            pl.BlockSpec((1, seq, LANE), lambda b, g, i: (b, 0, g)),
            pl.BlockSpec((1, seq, LANE), lambda b, g, i: (b, 0, g)),
        ],
        out_specs=pl.BlockSpec((1, GQA_TQ, LANE), lambda b, g, i: (b, i, g)),
        out_shape=jax.ShapeDtypeStruct((nb, seq, GQA_WIDTH), F32),
        compiler_params=_cparams(("parallel", "parallel", "arbitrary"), VMEM_LIMIT),
        name="gqa_flash",
    )(q3, k3, v3)


def _merge_kernel(h_ref, ya_ref, yb_ref, ys_ref, us_ref, yd_ref,
                  wg_ref, wb_ref, wo_ref, glw_ref, glb_ref, s5d_ref, g_ref, b_ref,
                  o_ref, op_ref):
    h = h_ref[...]
    hb = h.astype(BF16)
    y = ys_ref[...] + s5d_ref[...] * us_ref[...]
    c0 = math.sqrt(2.0 / math.pi)
    z = 0.5 * y * (1.0 + jnp.tanh(c0 * (y + 0.044715 * (y * y * y))))
    zl = jnp.dot(z.astype(BF16), glw_ref[...], preferred_element_type=F32) + glb_ref[...]
    yc = z * jax.nn.sigmoid(zl)
    merged = None
    for i, yv in enumerate((ya_ref[...], yb_ref[...], yc, yd_ref[...])):
        gate = jax.nn.sigmoid(jnp.dot(hb, wg_ref[:, i * D_MODEL:(i + 1) * D_MODEL],
                                      preferred_element_type=F32))
        term = gate * jnp.dot(yv.astype(BF16), wb_ref[i], preferred_element_type=F32)
        merged = term if merged is None else merged + term
    mix = jnp.dot(merged.astype(BF16), wo_ref[...], preferred_element_type=F32)
    out = _layer_norm(DN_ALPHA * h + mix, g_ref[...], b_ref[...])
    o_ref[...] = out
    half = D_MODEL // 2
    hi = pltpu.bitcast(out[:, :half].astype(BF16).astype(F32), U32)
    lo = pltpu.bitcast(out[:, half:].astype(BF16).astype(F32), U32)
    op_ref[...] = hi | (lo >> 16)


def merge_pallas(h, ya, yb, ys, proj, yd, wg, wb, wo, glw, glb, s5d, g, b, tm=256):
    n, d = h.shape
    bw = HY_WIDTH
    row = lambda i: (i, 0)
    fix2 = lambda i: (0, 0)
    return pl.pallas_call(
        _merge_kernel,
        grid=(n // tm,),
        in_specs=[
            pl.BlockSpec((tm, d), row),
            pl.BlockSpec((tm, bw), row),
            pl.BlockSpec((tm, bw), row),
            pl.BlockSpec((tm, bw), row),
            pl.BlockSpec((tm, bw), lambda i: (i, COL_S5 // bw)),
            pl.BlockSpec((tm, bw), row),
            pl.BlockSpec((d, N_BRANCHES * d), fix2),
            pl.BlockSpec((N_BRANCHES, bw, d), lambda i: (0, 0, 0)),
            pl.BlockSpec((d, d), fix2),
            pl.BlockSpec((bw, bw), fix2),
            pl.BlockSpec((1, bw), fix2),
            pl.BlockSpec((1, bw), fix2),
            pl.BlockSpec((1, d), fix2),
            pl.BlockSpec((1, d), fix2),
        ],
        out_specs=[pl.BlockSpec((tm, d), row), pl.BlockSpec((tm, d // 2), row)],
        out_shape=[jax.ShapeDtypeStruct((n, d), F32), jax.ShapeDtypeStruct((n, d // 2), U32)],
        compiler_params=_cparams(("parallel",), VMEM_LIMIT),
        name="gated_merge",
    )(h, ya, yb, ys, proj, yd, wg, wb, wo, glw, glb.reshape(1, bw), s5d.reshape(1, bw),
      g.reshape(1, d), b.reshape(1, d))


def _router_kernel(h_ref, w_ref, o_ref):
    h = h_ref[...]
    w = w_ref[...]
    h_hi = h.astype(BF16)
    h_lo = (h - h_hi.astype(F32)).astype(BF16)
    w_hi = w.astype(BF16)
    w_lo = (w - w_hi.astype(F32)).astype(BF16)
    nt = (((1,), (1,)), ((), ()))
    logits = (lax.dot_general(w_hi, h_hi, nt, preferred_element_type=F32)
              + lax.dot_general(w_hi, h_lo, nt, preferred_element_type=F32)
              + lax.dot_general(w_lo, h_hi, nt, preferred_element_type=F32))
    m = jnp.max(logits, axis=0, keepdims=True)
    e = jnp.exp(logits - m)
    o_ref[...] = e / jnp.sum(e, axis=0, keepdims=True)


def router_pallas(h, rw_t, tm=1024):
    n, d = h.shape
    ne = rw_t.shape[0]
    return pl.pallas_call(
        _router_kernel,
        grid=(n // tm,),
        in_specs=[pl.BlockSpec((tm, d), lambda i: (i, 0)), pl.BlockSpec((ne, d), lambda i: (0, 0))],
        out_specs=pl.BlockSpec((ne, tm), lambda i: (0, i)),
        out_shape=jax.ShapeDtypeStruct((ne, n), F32),
        compiler_params=_cparams(("parallel",)),
        name="router",
    )(h, rw_t)


def _topk_kernel(a_ref, tri_ref, idx_ref, g_ref, cs_scr, ga_scr, im_scr, gm_scr, *, seq, cap):
    aff = a_ref[...]
    ne = aff.shape[0]
    bits = pltpu.bitcast(aff, I32)
    tok = lax.broadcasted_iota(I32, aff.shape, 1)

    def count(mask):
        return jnp.sum(jnp.where(mask, 1.0, 0.0), axis=1, keepdims=True)

    def thr_step(it, thr):
        cand = thr | jnp.left_shift(jnp.int32(1), 30 - it)
        return jnp.where(count(bits >= cand) >= cap, cand, thr)

    thr = lax.fori_loop(0, 31, thr_step, jnp.zeros((ne, 1), I32))
    gt = bits > thr
    tie = bits == thr
    need = cap - count(gt)

    def tie_step(it, bound):
        cand = bound | jnp.left_shift(jnp.int32(1), 11 - it)
        return jnp.where(count(tie & (tok < cand)) < need, cand, bound)

    bound = lax.fori_loop(0, 12, tie_step, jnp.zeros((ne, 1), I32))
    sel = gt | (tie & (tok <= bound))
    sel_f = jnp.where(sel, 1.0, 0.0)

    nblk = seq // LANE
    stacked = jnp.concatenate([sel_f[:, j * LANE:(j + 1) * LANE] for j in range(nblk)], axis=0)
    pref = jnp.dot(stacked.astype(BF16), tri_ref[...], preferred_element_type=F32)
    off = jnp.zeros((ne, 1), F32)
    for j in range(nblk):
        pj = pref[j * ne:(j + 1) * ne, :]
        cs_scr[j] = pj + off
        ga_scr[j] = jnp.where(sel[:, j * LANE:(j + 1) * LANE], aff[:, j * LANE:(j + 1) * LANE], 0.0)
        off = off + pj[:, LANE - 1:LANE]
    rc = LANE
    jcol = lax.broadcasted_iota(I32, (rc, LANE), 0).astype(F32)
    lane = lax.broadcasted_iota(I32, (rc, LANE), 1)
    im_scr[...] = jnp.zeros_like(im_scr)
    gm_scr[...] = jnp.zeros_like(gm_scr)

    def per_expert(e, carry):
        for c in range(cap // rc):
            jc = jcol + float(c * rc)
            cnt = jnp.zeros((rc, LANE), F32)
            gacc = jnp.zeros((rc, LANE), F32)
            for j in range(nblk):
                cs_row = cs_scr[j, pl.ds(e, 1), :]
                ga_row = ga_scr[j, pl.ds(e, 1), :]
                cnt = cnt + jnp.where(cs_row <= jc, 1.0, 0.0)
                gacc = gacc + jnp.where(cs_row == jc + 1.0, ga_row, 0.0)
            rows = slice(c * rc, (c + 1) * rc)
            im_scr[rows, :] = jnp.where(lane == e, jnp.sum(cnt, axis=1, keepdims=True), im_scr[rows, :])
            gm_scr[rows, :] = jnp.where(lane == e, jnp.sum(gacc, axis=1, keepdims=True), gm_scr[rows, :])
        return carry

    lax.fori_loop(0, ne, per_expert, 0)
    idx_ref[0] = im_scr[...].T[0:ne, :].astype(I32)
    g_ref[0] = gm_scr[...].T[0:ne, :]


def topk_pallas(aff_t, tri, nb, seq, cap):
    ne = aff_t.shape[0]
    return pl.pallas_call(
        functools.partial(_topk_kernel, seq=seq, cap=cap),
        grid=(nb,),
        in_specs=[pl.BlockSpec((ne, seq), lambda b: (0, b)), pl.BlockSpec((LANE, LANE), lambda b: (0, 0))],
        out_specs=[pl.BlockSpec((1, ne, cap), lambda b: (b, 0, 0))] * 2,
        out_shape=[jax.ShapeDtypeStruct((nb, ne, cap), I32), jax.ShapeDtypeStruct((nb, ne, cap), F32)],
        scratch_shapes=[pltpu.VMEM((seq // LANE, ne, LANE), F32), pltpu.VMEM((seq // LANE, ne, LANE), F32),
                        pltpu.VMEM((cap, LANE), F32), pltpu.VMEM((cap, LANE), F32)],
        compiler_params=_cparams(("parallel",), VMEM_LIMIT),
        name="topk_select",
    )(aff_t, tri)


def _moe_kernel(idx_ref, gv_ref, hp_ref, wg_ref, wu_ref, wd_ref, o_ref, xs_ref, y_ref, *, cap):
    e = pl.program_id(1)

    @pl.when(e == 0)
    def _():
        o_ref[...] = jnp.zeros_like(o_ref)

    def gather(j, c):
        xs_ref[pl.ds(j, 1), :] = hp_ref[0, pl.ds(idx_ref[0, 0, j], 1), :]
        return c

    lax.fori_loop(0, cap, gather, 0)
    xp = xs_ref[...]
    x = jnp.concatenate([pltpu.bitcast(xp & jnp.uint32(0xFFFF0000), F32),
                         pltpu.bitcast(xp << 16, F32)], axis=1).astype(BF16)
    hg = jnp.dot(x, wg_ref[0], preferred_element_type=F32)
    hu = jnp.dot(x, wu_ref[0], preferred_element_type=F32)
    hid = (hg * jax.nn.sigmoid(hg) * hu).astype(BF16)
    y_ref[...] = jnp.dot(hid, wd_ref[0], preferred_element_type=F32)

    def scatter(j, c):
        t = idx_ref[0, 0, j]
        o_ref[0, pl.ds(t, 1), :] = o_ref[0, pl.ds(t, 1), :] + y_ref[pl.ds(j, 1), :] * gv_ref[0, 0, j]
        return c

    lax.fori_loop(0, cap, scatter, 0)


def moe_pallas(idx, gates, hp, wg, wu, wd, nb, seq, cap):
    ne, d, ff = wg.shape
    smem = functools.partial(pl.BlockSpec, memory_space=pltpu.SMEM)
    return pl.pallas_call(
        functools.partial(_moe_kernel, cap=cap),
        grid=(nb, ne),
        in_specs=[
            smem((1, 1, cap), lambda b, e: (b * ne + e, 0, 0)),
            smem((1, 1, cap), lambda b, e: (b * ne + e, 0, 0)),
            pl.BlockSpec((1, seq, d // 2), lambda b, e: (b, 0, 0), pipeline_mode=pl.Buffered(1)),
            pl.BlockSpec((1, d, ff), lambda b, e: (e, 0, 0)),
            pl.BlockSpec((1, d, ff), lambda b, e: (e, 0, 0)),
            pl.BlockSpec((1, ff, d), lambda b, e: (e, 0, 0)),
        ],
        out_specs=pl.BlockSpec((1, seq, d), lambda b, e: (b, 0, 0), pipeline_mode=pl.Buffered(1)),
        out_shape=jax.ShapeDtypeStruct((nb, seq, d), F32),
        scratch_shapes=[pltpu.VMEM((cap, d // 2), U32), pltpu.VMEM((cap, d), F32)],
        compiler_params=_cparams(("parallel", "arbitrary"), VMEM_LIMIT),
        name="moe_experts",
    )(idx, gates, hp, wg, wu, wd)


def _rope_tables(seq):
    n_rows = seq // GRID_W
    rows = jnp.repeat(jnp.arange(n_rows, dtype=F32), GRID_W)
    cols = (jnp.arange(seq) % GRID_W).astype(F32)
    half = HEAD_DIM // 2
    inv = ROPE_THETA ** (-jnp.arange(0, half, 2, dtype=F32) / half)
    ar, ac = rows[:, None] * inv, cols[:, None] * inv
    cos = jnp.concatenate([jnp.cos(ar), jnp.cos(ar), jnp.cos(ac), jnp.cos(ac)], axis=1)
    sin = jnp.concatenate([-jnp.sin(ar), jnp.sin(ar), -jnp.sin(ac), jnp.sin(ac)], axis=1)
    return jnp.tile(cos, (1, 2)), jnp.tile(sin, (1, 2))


def kernel(x, ln_in_g, ln_in_b, w_in, hy_conv_w, hy_conv_b, hy_w1, hy_b1, hy_w2, hy_b2, hy_w3, hy_freq, hy_bias,
           s5_lam_re, s5_lam_im, s5_log_dt, s5_b_re, s5_b_im, s5_c_re, s5_c_im, s5_d, s5_glu_w, s5_glu_b,
           gqa_q_norm, gqa_k_norm, w_branch, w_out, ln1_g, ln1_b,
           router_w, exp_w_gate, exp_w_up, exp_w_down, ln2_g, ln2_b):
    nb, seq, d = x.shape
    n = nb * seq
    cap = (EC_CAPACITY * seq) // N_EXPERTS
    assert cap == TOPK_JB
    nchunk = seq // S5_CHUNK

    cos_t, sin_t = _rope_tables(seq)
    slopes = jnp.asarray(2.0 ** (-8.0 * np.arange(1, DIL_N_HEADS + 1) / DIL_N_HEADS), F32)
    ones_bd = jnp.asarray(np.kron(np.eye(LANE // HEAD_DIM), np.ones((HEAD_DIM, HEAD_DIM))), BF16)
    tri = jnp.asarray(np.triu(np.ones((LANE, LANE))), BF16)

    h = ln_pallas(x.reshape(n, d), ln_in_g, ln_in_b)
    for l in range(DEPTH):
        w_mix = w_in[l][:, :N_MIX].astype(BF16)
        w_gate = w_in[l][:, N_MIX:].astype(BF16)
        proj = inproj_pallas(h, w_mix)

        kext = _hyena_filters_ext(seq, hy_w1[l], hy_b1[l], hy_w2[l], hy_b2[l], hy_w3[l], hy_freq[l])
        u_t = proj[:, :COL_DIL].reshape(nb, seq, COL_DIL).transpose(2, 0, 1)
        y_a = hyena_pallas(u_t, kext, hy_conv_w[l], hy_conv_b[l], hy_bias[l])
        y_a = y_a.transpose(1, 2, 0).reshape(n, HY_WIDTH)

        y_b = dilated_pallas(proj.reshape(nb, seq, N_MIX), slopes).reshape(n, 2 * LANE)

        m_mat, e_mat, f_mat, a_vec = _s5_matrices(s5_lam_re[l], s5_lam_im[l], s5_log_dt[l],
                                                  s5_b_re[l], s5_b_im[l], s5_c_re[l], s5_c_im[l])
        u_g = proj[:, COL_S5:COL_S5 + S5_WIDTH].reshape(nb, nchunk, S5_CHUNK, S5_N_GROUPS, S5_GROUP)
        u_g = u_g.transpose(3, 1, 0, 2, 4).reshape(S5_N_GROUPS, nchunk * nb, S5_CHUNK * S5_GROUP)
        y_s = s5_pallas(u_g, m_mat, e_mat, f_mat, a_vec, nb)
        y_s = y_s.reshape(S5_N_GROUPS, nchunk, nb, S5_CHUNK, S5_GROUP).transpose(2, 1, 3, 0, 4)
        y_s = y_s.reshape(n, S5_WIDTH)

        qn = jnp.tile(gqa_q_norm[l], LANE // HEAD_DIM).reshape(1, LANE)
        kn = jnp.tile(gqa_k_norm[l], LANE // HEAD_DIM).reshape(1, LANE)
        qr, kd, vd = gqa_prep_pallas(proj, cos_t, sin_t, qn, kn, ones_bd, seq)
        y_d = gqa_pallas(qr, kd, vd, nb, seq).reshape(n, GQA_WIDTH)

        h, hp = merge_pallas(h, y_a, y_b, y_s, proj, y_d, w_gate, w_branch[l].astype(BF16),
                             w_out[l].astype(BF16), s5_glu_w[l].astype(BF16), s5_glu_b[l], s5_d[l],
                             ln1_g[l], ln1_b[l])

        aff_t = router_pallas(h, router_w[l].T)
        idx, gates = topk_pallas(aff_t, tri, nb, seq, cap)
        ffn = moe_pallas(idx.reshape(nb * N_EXPERTS, 1, cap), gates.reshape(nb * N_EXPERTS, 1, cap),
                         hp.reshape(nb, seq, d // 2), exp_w_gate[l].astype(BF16), exp_w_up[l].astype(BF16),
                         exp_w_down[l].astype(BF16), nb, seq, cap)
        h = res_ln_pallas(h, ffn.reshape(n, d), ln2_g[l], ln2_b[l])
    return h.reshape(nb, seq, d)
```

```python
import functools
import math

import jax
import jax.numpy as jnp
import numpy as np
from jax import lax
from jax.experimental import pallas as pl
from jax.experimental.pallas import tpu as pltpu

F32 = jnp.float32
BF16 = jnp.bfloat16
I32 = jnp.int32
U32 = jnp.uint32

D_MODEL = 1024
DEPTH = 4
HEAD_DIM = 64
GRID_W = 64
N_BRANCHES = 4
LN_EPS = 1e-5
RMS_EPS = 1e-6

HY_WIDTH = 256
HY_ORDER = 2
HY_POS_EMB = 33
HY_TARGET = 1e-2
HY_SHORT_DECAY_PCT = 0.3
HY_LONG_DECAY_PCT = 1.5

DIL_GROUPS = ((128, 1), (512, 4), (2048, 16))
DIL_HEADS_PER_GROUP = 4
DIL_N_HEADS = 12
DIL_WIDTH = DIL_N_HEADS * HEAD_DIM

S5_WIDTH = 256
S5_GROUP = 16
S5_N_GROUPS = 16
S5_STATE = 64

GQA_HEADS = 4
GQA_KV_HEADS = 2
GQA_WIDTH = GQA_HEADS * HEAD_DIM
GQA_KV_WIDTH = GQA_KV_HEADS * HEAD_DIM
ROPE_THETA = 10000.0

N_EXPERTS = 16
EC_CAPACITY = 2
EXPERT_FF = 1024

DN_ALPHA = (2 * DEPTH) ** 0.25

COL_HY = 0
COL_DIL = 3 * HY_WIDTH
COL_S5 = COL_DIL + 3 * DIL_WIDTH
COL_GQ = COL_S5 + S5_WIDTH
COL_GK = COL_GQ + GQA_WIDTH
COL_GV = COL_GK + GQA_KV_WIDTH
N_MIX = COL_GV + GQA_KV_WIDTH
N_IN = N_MIX + N_BRANCHES * D_MODEL

LANE = 128
SUBLANE = 8
VMEM_LIMIT = 56 * 1024 * 1024

HY_CB = 8
HY_BLK = 256
TOPK_JB = 512


def _cparams(sem, vmem=None):
    return pltpu.CompilerParams(dimension_semantics=sem, vmem_limit_bytes=vmem)


def _layer_norm(x, g, b):
    mu = jnp.mean(x, axis=-1, keepdims=True)
    xc = x - mu
    var = jnp.mean(xc * xc, axis=-1, keepdims=True)
    return xc * lax.rsqrt(var + LN_EPS) * g + b


def _ln_kernel(x_ref, g_ref, b_ref, o_ref):
    o_ref[...] = _layer_norm(x_ref[...], g_ref[...], b_ref[...])


def ln_pallas(x, g, b, tm=1024):
    n, d = x.shape
    return pl.pallas_call(
        _ln_kernel,
        grid=(n // tm,),
        in_specs=[pl.BlockSpec((tm, d), lambda i: (i, 0)),
                  pl.BlockSpec((1, d), lambda i: (0, 0)),
                  pl.BlockSpec((1, d), lambda i: (0, 0))],
        out_specs=pl.BlockSpec((tm, d), lambda i: (i, 0)),
        out_shape=jax.ShapeDtypeStruct((n, d), F32),
        compiler_params=_cparams(("parallel",)),
        name="ln_in",
    )(x, g.reshape(1, d), b.reshape(1, d))


def _res_ln_kernel(h_ref, f_ref, g_ref, b_ref, o_ref):
    o_ref[...] = _layer_norm(DN_ALPHA * h_ref[...] + f_ref[...], g_ref[...], b_ref[...])


def res_ln_pallas(h, f, g, b, tm=1024):
    n, d = h.shape
    return pl.pallas_call(
        _res_ln_kernel,
        grid=(n // tm,),
        in_specs=[pl.BlockSpec((tm, d), lambda i: (i, 0)),
                  pl.BlockSpec((tm, d), lambda i: (i, 0)),
                  pl.BlockSpec((1, d), lambda i: (0, 0)),
                  pl.BlockSpec((1, d), lambda i: (0, 0))],
        out_specs=pl.BlockSpec((tm, d), lambda i: (i, 0)),
        out_shape=jax.ShapeDtypeStruct((n, d), F32),
        compiler_params=_cparams(("parallel",)),
        name="res_ln",
    )(h, f, g.reshape(1, d), b.reshape(1, d))


def _inproj_kernel(h_ref, w_ref, o_ref):
    o_ref[...] = jnp.dot(h_ref[...].astype(BF16), w_ref[...], preferred_element_type=F32)


def inproj_pallas(h, w_bf16, tm=1024, tn=1280):
    n, d = h.shape
    nout = w_bf16.shape[1]
    return pl.pallas_call(
        _inproj_kernel,
        grid=(nout // tn, n // tm),
        in_specs=[pl.BlockSpec((tm, d), lambda j, i: (i, 0)),
                  pl.BlockSpec((d, tn), lambda j, i: (0, j))],
        out_specs=pl.BlockSpec((tm, tn), lambda j, i: (i, j)),
        out_shape=jax.ShapeDtypeStruct((n, nout), F32),
        compiler_params=_cparams(("parallel", "parallel"), VMEM_LIMIT),
        name="in_proj",
    )(h, w_bf16)


def _hy_kernel(cw_ref, cb_ref, hb_ref,
               v_ref, x1_ref, x2_ref, k_ref,
               o_ref,
               f2_ref, ua_ref, ub_ref,
               *, seq, nb):
    g = pl.program_id(0)
    nblk = seq // HY_BLK
    npad = nblk - 1
    nlb = 2 * seq // LANE
    rows = nb * nblk

    lane_i = lax.broadcasted_iota(I32, (LANE, LANE), 1)
    row_i = lax.broadcasted_iota(I32, (LANE, LANE), 0)
    keep_cur = lane_i >= row_i
    t_i = lax.broadcasted_iota(I32, (nb, seq), 1)

    def short_conv(x, ch):
        w0 = cw_ref[ch]
        w1 = cw_ref[3 * HY_WIDTH + ch]
        w2 = cw_ref[6 * HY_WIDTH + ch]
        xm = jnp.where(t_i == 0, 0.0, pltpu.roll(x, 1, 1))
        xp = jnp.where(t_i == seq - 1, 0.0, pltpu.roll(x, seq - 1, 1))
        return xm * w0 + x * w1 + xp * w2 + cb_ref[ch]

    def long_conv(u, order, ci):
        krow = k_ref[order, pl.ds(ci, 1), :]
        prev = None
        for n in range(nlb):
            blk = jnp.broadcast_to(krow[:, n * LANE:(n + 1) * LANE], (LANE, LANE))
            cur = pltpu.roll(blk, 0, 1, stride=1, stride_axis=0)
            if n >= 1:
                out = jnp.where(keep_cur, cur, prev).astype(BF16)
                f2_ref[0:LANE, n * LANE:(n + 1) * LANE] = out
                if n + 1 < nlb:
                    f2_ref[LANE:2 * LANE, (n + 1) * LANE:(n + 2) * LANE] = out
            prev = cur
        blocks = [u[:, j * HY_BLK:(j + 1) * HY_BLK] for j in range(nblk)]
        za = jnp.zeros((npad * nb, HY_BLK), F32)
        ua = jnp.concatenate([za] + blocks + [za, jnp.zeros((2 * nb, HY_BLK), F32)], axis=0)
        zb = jnp.zeros(((npad - 1) * nb, HY_BLK), F32)
        ub = jnp.concatenate([zb] + blocks + [za, jnp.zeros((3 * nb, HY_BLK), F32)], axis=0)
        ua_ref[...] = ua.astype(BF16)
        ub_ref[...] = ub.astype(BF16)
        acc = jnp.zeros((rows, HY_BLK), F32)
        for d in range(-npad, npad + 1):
            blk0 = npad - d
            if blk0 % 2 == 0:
                lhs = ua_ref[blk0 * nb:blk0 * nb + rows, :]
            else:
                lhs = ub_ref[(blk0 - 1) * nb:(blk0 - 1) * nb + rows, :]
            c0 = seq + HY_BLK * d
            acc = acc + jnp.dot(lhs, f2_ref[:, c0:c0 + HY_BLK], preferred_element_type=F32)
        return jnp.concatenate([acc[j * nb:(j + 1) * nb, :] for j in range(nblk)], axis=1)

    def body(ci, carry):
        ch = g * HY_CB + ci
        v = short_conv(v_ref[ci], ch)
        x1 = short_conv(x1_ref[ci], HY_WIDTH + ch)
        x2 = short_conv(x2_ref[ci], 2 * HY_WIDTH + ch)
        z = x1 * (long_conv(v, 0, ci) + v * hb_ref[ch])
        z = x2 * (long_conv(z, 1, ci) + z * hb_ref[HY_WIDTH + ch])
        o_ref[ci] = z
        return carry

    lax.fori_loop(0, HY_CB, body, 0)


def hyena_pallas(u_t, kext, conv_w, conv_b, hy_bias):
    _, nb, seq = u_t.shape
    assert nb == SUBLANE and seq % HY_BLK == 0
    nblk = seq // HY_BLK
    ncb = HY_WIDTH // HY_CB
    pad_rows = (2 * (nblk - 1) + nblk + 2) * nb
    grid_spec = pltpu.PrefetchScalarGridSpec(
        num_scalar_prefetch=3,
        grid=(ncb,),
        in_specs=[
            pl.BlockSpec((HY_CB, nb, seq), lambda g, *_: (g, 0, 0)),
            pl.BlockSpec((HY_CB, nb, seq), lambda g, *_: (ncb + g, 0, 0)),
            pl.BlockSpec((HY_CB, nb, seq), lambda g, *_: (2 * ncb + g, 0, 0)),
            pl.BlockSpec((HY_ORDER, HY_CB, 2 * seq), lambda g, *_: (0, g, 0)),
        ],
        out_specs=pl.BlockSpec((HY_CB, nb, seq), lambda g, *_: (g, 0, 0)),
        scratch_shapes=[
            pltpu.VMEM((HY_BLK, 2 * seq), BF16),
            pltpu.VMEM((pad_rows, HY_BLK), BF16),
            pltpu.VMEM((pad_rows, HY_BLK), BF16),
        ],
    )
    return pl.pallas_call(
        functools.partial(_hy_kernel, seq=seq, nb=nb),
        grid_spec=grid_spec,
        out_shape=jax.ShapeDtypeStruct((HY_WIDTH, nb, seq), F32),
        compiler_params=_cparams(("arbitrary",)),
        name="hyena_conv",
    )(conv_w.reshape(-1), conv_b, hy_bias.reshape(-1), u_t, u_t, u_t, kext)


def _hyena_filters_ext(seq, w1, b1, w2, b2, w3, freq):
    hp = lax.Precision.HIGHEST
    t = jnp.linspace(0.0, 1.0, seq, dtype=F32)[:, None]
    bands = (HY_POS_EMB - 1) // 2
    f = jnp.linspace(1e-4, bands - 1, bands, dtype=F32)[None, :]
    w = 2.0 * math.pi * jnp.arange(seq, dtype=F32)[:, None] / seq
    z = jnp.concatenate([t, jnp.cos(f * w), -jnp.sin(f * w)], axis=-1)
    h = jnp.sin(freq[0] * (jnp.dot(z, w1, precision=hp) + b1))
    h = jnp.sin(freq[1] * (jnp.dot(h, w2, precision=hp) + b2))
    h = jnp.dot(h, w3, precision=hp).reshape(seq, 2, HY_ORDER, HY_WIDTH)
    max_decay = math.log(HY_TARGET) / HY_SHORT_DECAY_PCT
    min_decay = math.log(HY_TARGET) / HY_LONG_DECAY_PCT
    deltas = jnp.abs(jnp.linspace(min_decay, max_decay, HY_WIDTH, dtype=F32))
    h = h * jnp.exp(-t * deltas)[:, None, None, :]
    fwd, bwd = h[:, 0], h[:, 1]
    k = jnp.concatenate([fwd, jnp.zeros_like(fwd[:1]), bwd[:0:-1]], axis=0)
    k = k * lax.rsqrt(jnp.sum(k * k, axis=0, keepdims=True))
    return jnp.roll(k, seq, axis=0).transpose(1, 2, 0)


DIL_QB = 128
DIL_KB = 256
DIL_HALF = 64


def _dil_kernel(slope_ref, q_ref, k_ref, v_ref, o_ref, oacc_ref, lse_ref, *, seq):
    pj = pl.program_id(1)
    g = pl.program_id(2)
    head1 = lax.broadcasted_iota(I32, (DIL_QB, LANE), 1) >= HEAD_DIM
    qi = lax.broadcasted_iota(I32, (DIL_QB, DIL_KB), 0)
    ki = lax.broadcasted_iota(I32, (DIL_QB, DIL_KB), 1)
    scale = HEAD_DIM ** -0.5

    for gi, (w, d) in enumerate(DIL_GROUPS):
        assert w // (2 * d) == DIL_HALF
        ln = seq // d

        @pl.when(g == gi)
        def _(gi=gi, d=d, ln=ln):
            def task(t, carry):
                r = t % d
                mq = (t // d) * DIL_QB
                mk = jnp.clip(mq - DIL_HALF, 0, ln - DIL_KB)
                q = q_ref[0, pl.ds(r + d * mq, DIL_QB, stride=d), :]
                k = k_ref[0, pl.ds(r + d * mk, DIL_KB, stride=d), :].astype(BF16)
                v = v_ref[0, pl.ds(r + d * mk, DIL_KB, stride=d), :].astype(BF16)
                rel = jnp.abs((ki + mk) - (qi + mq))
                valid = rel <= DIL_HALF
                dist = (rel * d).astype(F32)
                outs, lses = [], []
                for hh in range(2):
                    slope = slope_ref[4 * gi + 2 * pj + hh]
                    qm = jnp.where(head1 == (hh == 1), q, 0.0).astype(BF16)
                    s = lax.dot_general(qm, k, (((1,), (1,)), ((), ())),
                                        preferred_element_type=F32)
                    s = s * scale - slope * dist
                    s = jnp.where(valid, s, -jnp.inf)
                    m = jnp.max(s, axis=1, keepdims=True)
                    p = jnp.exp(s - m)
                    l = jnp.sum(p, axis=1, keepdims=True)
                    o = jnp.dot(p.astype(BF16), v, preferred_element_type=F32)
                    outs.append(o / l)
                    lses.append(m + jnp.log(l))
                o = jnp.where(head1, outs[1], outs[0])
                lse = jnp.where(head1, lses[1], lses[0])
                oacc_ref[gi, pl.ds(r + d * mq, DIL_QB, stride=d), :] = o
                lse_ref[gi, pl.ds(r + d * mq, DIL_QB, stride=d), :] = lse
                return carry

            lax.fori_loop(0, seq // DIL_QB, task, 0, unroll=4)

    @pl.when(g == len(DIL_GROUPS) - 1)
    def _():
        l0, l1, l2 = lse_ref[0], lse_ref[1], lse_ref[2]
        m = jnp.maximum(jnp.maximum(l0, l1), l2)
        e0, e1, e2 = jnp.exp(l0 - m), jnp.exp(l1 - m), jnp.exp(l2 - m)
        num = e0 * oacc_ref[0] + e1 * oacc_ref[1] + e2 * oacc_ref[2]
        o_ref[0] = num / (e0 + e1 + e2)


def dilated_pallas(proj3, slopes):
    nb, seq, _ = proj3.shape
    ng = len(DIL_GROUPS)
    qb, kb, vb = COL_DIL // LANE, (COL_DIL + DIL_WIDTH) // LANE, (COL_DIL + 2 * DIL_WIDTH) // LANE
    grid_spec = pltpu.PrefetchScalarGridSpec(
        num_scalar_prefetch=1,
        grid=(nb, 2, ng),
        in_specs=[
            pl.BlockSpec((1, seq, LANE), lambda b, p, g, *_: (b, 0, qb + 2 * g + p)),
            pl.BlockSpec((1, seq, LANE), lambda b, p, g, *_: (b, 0, kb + 2 * g + p)),
            pl.BlockSpec((1, seq, LANE), lambda b, p, g, *_: (b, 0, vb + 2 * g + p)),
        ],
        out_specs=pl.BlockSpec((1, seq, LANE), lambda b, p, g, *_: (b, 0, p)),
        scratch_shapes=[pltpu.VMEM((ng, seq, LANE), F32), pltpu.VMEM((ng, seq, LANE), F32)],
    )
    return pl.pallas_call(
        functools.partial(_dil_kernel, seq=seq),
        grid_spec=grid_spec,
        out_shape=jax.ShapeDtypeStruct((nb, seq, 2 * LANE), F32),
        compiler_params=_cparams(("parallel", "parallel", "arbitrary"), VMEM_LIMIT),
        name="dilated_attn",
    )(slopes, proj3, proj3, proj3)


S5_NS = S5_N_GROUPS * S5_STATE


def _s5_kernel(xf_ref, xb_ref, bd_ref, cd_ref, a_ref, yf_ref, yb_ref, buf_f, buf_b, st_ref, *, nb, tt):
    @pl.when(pl.program_id(0) == 0)
    def _():
        st_ref[...] = jnp.zeros_like(st_ref)

    width = xf_ref.shape[-1]
    nlb = S5_NS // LANE

    def project(x_ref, d, buf):
        bu = jnp.dot(x_ref[...].reshape(nb * tt, width).astype(BF16), bd_ref[d], preferred_element_type=F32)
        for c in range(2 * nlb):
            buf[c] = bu[:, c * LANE:(c + 1) * LANE]

    project(xf_ref, 0, buf_f)
    project(xb_ref, 1, buf_b)

    def advance(buf, t, d, state):
        rows = pl.ds(t, nb, stride=tt)
        out = []
        for c in range(nlb):
            lanes = slice(c * LANE, (c + 1) * LANE)
            a_re = a_ref[2 * d, :, lanes]
            a_im = a_ref[2 * d + 1, :, lanes]
            x_re, x_im = state[2 * c], state[2 * c + 1]
            n_re = a_re * x_re - a_im * x_im + buf[c, rows, :]
            n_im = a_re * x_im + a_im * x_re + buf[nlb + c, rows, :]
            buf[c, rows, :] = n_re
            buf[nlb + c, rows, :] = n_im
            out += [n_re, n_im]
        return tuple(out)

    def step(t, carry):
        return (advance(buf_f, t, 0, carry[0]), advance(buf_b, tt - 1 - t, 1, carry[1]))

    def load_state(d):
        return tuple(st_ref[2 * d + (i % 2), :, (i // 2) * LANE:(i // 2 + 1) * LANE] for i in range(2 * nlb))

    carry = lax.fori_loop(0, tt, step, (load_state(0), load_state(1)))
    for d in range(2):
        for i in range(2 * nlb):
            st_ref[2 * d + (i % 2), :, (i // 2) * LANE:(i // 2 + 1) * LANE] = carry[d][i]

    def readout(buf, d, y_ref):
        xs = jnp.concatenate([buf[c] for c in range(2 * nlb)], axis=1).astype(BF16)
        y_ref[...] = jnp.dot(xs, cd_ref[d], preferred_element_type=F32).reshape(nb, tt, width)

    readout(buf_f, 0, yf_ref)
    readout(buf_b, 1, yb_ref)


def s5_pallas(proj3, bd, cd, a_b, tt=64):
    nb, seq, _ = proj3.shape
    nt = seq // tt
    w = S5_WIDTH
    cb = COL_S5 // w
    sd = jax.ShapeDtypeStruct
    return pl.pallas_call(
        functools.partial(_s5_kernel, nb=nb, tt=tt),
        grid=(nt,),
        in_specs=[
            pl.BlockSpec((nb, tt, w), lambda k: (0, k, cb)),
            pl.BlockSpec((nb, tt, w), lambda k: (0, nt - 1 - k, cb)),
            pl.BlockSpec((2, w, 2 * S5_NS), lambda k: (0, 0, 0)),
            pl.BlockSpec((2, 2 * S5_NS, w), lambda k: (0, 0, 0)),
            pl.BlockSpec((4, nb, S5_NS), lambda k: (0, 0, 0)),
        ],
        out_specs=[pl.BlockSpec((nb, tt, w), lambda k: (0, k, 0)),
                   pl.BlockSpec((nb, tt, w), lambda k: (0, nt - 1 - k, 0))],
        out_shape=[sd((nb, seq, w), F32), sd((nb, seq, w), F32)],
        scratch_shapes=[pltpu.VMEM((2 * S5_NS // LANE, nb * tt, LANE), F32),
                        pltpu.VMEM((2 * S5_NS // LANE, nb * tt, LANE), F32),
                        pltpu.VMEM((4, nb, S5_NS), F32)],
        compiler_params=_cparams(("arbitrary",), VMEM_LIMIT),
        name="s5_scan",
    )(proj3, proj3, bd, cd, a_b)


def _s5_operators(lam_re, lam_im, log_dt, b_re, b_im, c_re, c_im, nb):
    dt = jnp.exp(log_dt)[..., None]
    zr, zi = lam_re * dt, lam_im * dt
    mag = jnp.exp(zr)
    ar, ai = mag * jnp.cos(zi), mag * jnp.sin(zi)
    den = lam_re * lam_re + lam_im * lam_im
    qr = ((ar - 1.0) * lam_re + ai * lam_im) / den
    qi = (ai * lam_re - (ar - 1.0) * lam_im) / den
    bbr = qr[..., None] * b_re - qi[..., None] * b_im
    bbi = qr[..., None] * b_im + qi[..., None] * b_re
    eye = jnp.eye(S5_N_GROUPS, dtype=F32)
    bd_r = jnp.einsum('dgpc,gh->dgchp', bbr, eye).reshape(2, S5_WIDTH, S5_NS)
    bd_i = jnp.einsum('dgpc,gh->dgchp', bbi, eye).reshape(2, S5_WIDTH, S5_NS)
    bd = jnp.concatenate([bd_r, bd_i], axis=2)
    cd_r = jnp.einsum('dgcp,gh->dgphc', c_re, eye).reshape(2, S5_NS, S5_WIDTH)
    cd_i = jnp.einsum('dgcp,gh->dgphc', c_im, eye).reshape(2, S5_NS, S5_WIDTH)
    cd = jnp.concatenate([cd_r, -cd_i], axis=1)
    a_b = jnp.stack([ar[0], ai[0], ar[1], ai[1]], axis=0).reshape(4, 1, S5_NS)
    return bd.astype(BF16), cd.astype(BF16), jnp.broadcast_to(a_b, (4, nb, S5_NS))


def _gqa_prep_kernel(q_ref, k_ref, v_ref, cos_ref, sin_ref, qn_ref, kn_ref, ones_ref, qo_ref, ko_ref, vo_ref):
    lane = lax.broadcasted_iota(I32, cos_ref.shape, 1)
    first16 = (lane & 31) < 16
    cos = cos_ref[...]
    sin = sin_ref[...]
    ones = ones_ref[...]

    def rms(x, gain):
        sq = x * x
        hi = sq.astype(BF16)
        lo = (sq - hi.astype(F32)).astype(BF16)
        ms = (jnp.dot(hi, ones, preferred_element_type=F32)
              + jnp.dot(lo, ones, preferred_element_type=F32)) * (1.0 / HEAD_DIM)
        return x * lax.rsqrt(ms + RMS_EPS) * gain

    def rope(x):
        swapped = jnp.where(first16, pltpu.roll(x, LANE - 16, 1), pltpu.roll(x, 16, 1))
        return x * cos + swapped * sin

    scale = HEAD_DIM ** -0.5 * math.log2(math.e)
    q = q_ref[...]
    qn = qn_ref[...]
    for g in range(GQA_KV_HEADS):
        qg = rope(rms(q[:, g * LANE:(g + 1) * LANE], qn)) * scale
        sw = pltpu.roll(qg, HEAD_DIM, 1)
        keep = (lane >= g * HEAD_DIM) & (lane < (g + 1) * HEAD_DIM)
        for r in range(2):
            src = qg if r == g else sw
            qo_ref[0, 2 * g + r] = jnp.where(keep, src, 0.0).T.astype(BF16)
    ko_ref[...] = rope(rms(k_ref[...], kn_ref[...])).astype(BF16)
    vo_ref[0] = v_ref[...].T.astype(BF16)


def gqa_prep_pallas(proj, cos_t, sin_t, qn, kn, ones_bd, seq, tm=1024):
    n = proj.shape[0]
    nsb = seq // tm
    nb = n // seq
    w = GQA_WIDTH
    sd = jax.ShapeDtypeStruct
    return pl.pallas_call(
        _gqa_prep_kernel,
        grid=(n // tm,),
        in_specs=[
            pl.BlockSpec((tm, w), lambda i: (i, COL_GQ // w)),
            pl.BlockSpec((tm, LANE), lambda i: (i, COL_GK // LANE)),
            pl.BlockSpec((tm, LANE), lambda i: (i, COL_GV // LANE)),
            pl.BlockSpec((tm, LANE), lambda i: (i % nsb, 0)),
            pl.BlockSpec((tm, LANE), lambda i: (i % nsb, 0)),
            pl.BlockSpec((1, LANE), lambda i: (0, 0)),
            pl.BlockSpec((1, LANE), lambda i: (0, 0)),
            pl.BlockSpec((LANE, LANE), lambda i: (0, 0)),
        ],
        out_specs=[pl.BlockSpec((1, GQA_HEADS, LANE, tm), lambda i: (i // nsb, 0, 0, i % nsb)),
                   pl.BlockSpec((tm, LANE), lambda i: (i, 0)),
                   pl.BlockSpec((1, LANE, tm), lambda i: (i // nsb, 0, i % nsb))],
        out_shape=[sd((nb, GQA_HEADS, LANE, seq), BF16), sd((n, LANE), BF16), sd((nb, LANE, seq), BF16)],
        compiler_params=_cparams(("parallel",)),
        name="gqa_prep",
    )(proj, proj, proj, cos_t, sin_t, qn, kn, ones_bd)


GQA_TQ = 256
GQA_KC = 1024
GQA_AHEAD = 4


def _col_reduce(x, op):
    rows, cols = x.shape
    x = op(x.reshape(rows // (8 * SUBLANE), 8, SUBLANE, cols), axis=1)
    x = op(x, axis=0)
    return op(x, axis=0, keepdims=True)


def _gqa_kernel(q_ref, k_ref, v_ref, o_ref, *, seq):
    nchunk = seq // GQA_KC
    jobs = [(r, c) for c in range(nchunk) for r in range(2)]

    def scores(job):
        r, c = job
        k = k_ref[0, c * GQA_KC:(c + 1) * GQA_KC, :]
        return jnp.dot(k, q_ref[0, r], preferred_element_type=F32)

    m = [jnp.full((1, GQA_TQ), -jnp.inf, F32)] * 2
    l = [jnp.zeros((1, GQA_TQ), F32)] * 2
    acc = [jnp.zeros((HEAD_DIM, GQA_TQ), F32)] * 2
    pend = [scores(job) for job in jobs[:GQA_AHEAD]]
    for i, (r, c) in enumerate(jobs):
        s = pend.pop(0)
        if i + GQA_AHEAD < len(jobs):
            pend.append(scores(jobs[i + GQA_AHEAD]))
        v_t = v_ref[0, :, c * GQA_KC:(c + 1) * GQA_KC]
        mn = jnp.maximum(m[r], _col_reduce(s, jnp.max))
        alpha = jnp.exp2(m[r] - mn)
        p = jnp.exp2(s - mn)
        l[r] = alpha * l[r] + _col_reduce(p, jnp.sum)
        acc[r] = alpha * acc[r] + jnp.dot(v_t, p.astype(BF16), preferred_element_type=F32)
        m[r] = mn
    outs = [acc[r] / l[r] for r in range(2)]
    o_ref[0] = jnp.concatenate(outs, axis=0).T


def gqa_pallas(q_t, kr, v_t, nb, seq):
    k3 = kr.reshape(nb, seq, LANE)
    return pl.pallas_call(
        functools.partial(_gqa_kernel, seq=seq),
        grid=(nb, GQA_KV_HEADS, seq // GQA_TQ),
        in_specs=[
            pl.BlockSpec((1, 2, LANE, GQA_TQ), lambda b, g, i: (b, g, 0, i)),
            pl.BlockSpec((1, seq, LANE), lambda b, g, i: (b, 0, 0)),
            pl.BlockSpec((1, HEAD_DIM, seq), lambda b, g, i: (b, g, 0)),
        ],
        out_specs=pl.BlockSpec((1, GQA_TQ, LANE), lambda b, g, i: (b, i, g)),
        out_shape=jax.ShapeDtypeStruct((nb, seq, GQA_WIDTH), F32),
        compiler_params=_cparams(("parallel", "parallel", "arbitrary"), VMEM_LIMIT),
        name="gqa_flash",
    )(q_t, k3, v_t)


def _merge_kernel(h_ref, ya_ref, yb_ref, ysf_ref, ysb_ref, us_ref, yd_ref,
                  wg_ref, wb_ref, wo_ref, glw_ref, glb_ref, s5d_ref, g_ref, b_ref,
                  o_ref, op_ref):
    h = h_ref[...]
    hb = h.astype(BF16)
    y = ysf_ref[...] + ysb_ref[...] + s5d_ref[...] * us_ref[...]
    c0 = math.sqrt(2.0 / math.pi)
    z = 0.5 * y * (1.0 + jnp.tanh(c0 * (y + 0.044715 * (y * y * y))))
    zl = jnp.dot(z.astype(BF16), glw_ref[...], preferred_element_type=F32) + glb_ref[...]
    yc = z * jax.nn.sigmoid(zl)
    merged = None
    for i, yv in enumerate((ya_ref[...], yb_ref[...], yc, yd_ref[...])):
        gate = jax.nn.sigmoid(jnp.dot(hb, wg_ref[:, i * D_MODEL:(i + 1) * D_MODEL],
                                      preferred_element_type=F32))
        term = gate * jnp.dot(yv.astype(BF16), wb_ref[i], preferred_element_type=F32)
        merged = term if merged is None else merged + term
    mix = jnp.dot(merged.astype(BF16), wo_ref[...], preferred_element_type=F32)
    out = _layer_norm(DN_ALPHA * h + mix, g_ref[...], b_ref[...])
    o_ref[...] = out
    half = D_MODEL // 2
    hi = pltpu.bitcast(out[:, :half].astype(BF16).astype(F32), U32)
    lo = pltpu.bitcast(out[:, half:].astype(BF16).astype(F32), U32)
    op_ref[...] = hi | (lo >> 16)


def merge_pallas(h, ya, yb, ysf, ysb, proj, yd, wg, wb, wo, glw, glb, s5d, g, b, tm=256):
    n, d = h.shape
    bw = HY_WIDTH
    row = lambda i: (i, 0)
    fix2 = lambda i: (0, 0)
    return pl.pallas_call(
        _merge_kernel,
        grid=(n // tm,),
        in_specs=[
            pl.BlockSpec((tm, d), row),
            pl.BlockSpec((tm, bw), row),
            pl.BlockSpec((tm, bw), row),
            pl.BlockSpec((tm, bw), row),
            pl.BlockSpec((tm, bw), row),
            pl.BlockSpec((tm, bw), lambda i: (i, COL_S5 // bw)),
            pl.BlockSpec((tm, bw), row),
            pl.BlockSpec((d, N_BRANCHES * d), fix2),
            pl.BlockSpec((N_BRANCHES, bw, d), lambda i: (0, 0, 0)),
            pl.BlockSpec((d, d), fix2),
            pl.BlockSpec((bw, bw), fix2),
            pl.BlockSpec((1, bw), fix2),
            pl.BlockSpec((1, bw), fix2),
            pl.BlockSpec((1, d), fix2),
            pl.BlockSpec((1, d), fix2),
        ],
        out_specs=[pl.BlockSpec((tm, d), row), pl.BlockSpec((tm, d // 2), row)],
        out_shape=[jax.ShapeDtypeStruct((n, d), F32), jax.ShapeDtypeStruct((n, d // 2), U32)],
        compiler_params=_cparams(("parallel",), VMEM_LIMIT),
        name="gated_merge",
    )(h, ya, yb, ysf, ysb, proj, yd, wg, wb, wo, glw, glb.reshape(1, bw), s5d.reshape(1, bw),
      g.reshape(1, d), b.reshape(1, d))


def _router_kernel(h_ref, w_ref, o_ref):
    h = h_ref[...]
    w = w_ref[...]
    h_hi = h.astype(BF16)
    h_lo = (h - h_hi.astype(F32)).astype(BF16)
    w_hi = w.astype(BF16)
    w_lo = (w - w_hi.astype(F32)).astype(BF16)
    nt = (((1,), (1,)), ((), ()))
    logits = (lax.dot_general(w_hi, h_hi, nt, preferred_element_type=F32)
              + lax.dot_general(w_hi, h_lo, nt, preferred_element_type=F32)
              + lax.dot_general(w_lo, h_hi, nt, preferred_element_type=F32))
    m = jnp.max(logits, axis=0, keepdims=True)
    e = jnp.exp(logits - m)
    o_ref[...] = e / jnp.sum(e, axis=0, keepdims=True)


def router_pallas(h, rw_t, tm=1024):
    n, d = h.shape
    ne = rw_t.shape[0]
    return pl.pallas_call(
        _router_kernel,
        grid=(n // tm,),
        in_specs=[pl.BlockSpec((tm, d), lambda i: (i, 0)), pl.BlockSpec((ne, d), lambda i: (0, 0))],
        out_specs=pl.BlockSpec((ne, tm), lambda i: (0, i)),
        out_shape=jax.ShapeDtypeStruct((ne, n), F32),
        compiler_params=_cparams(("parallel",)),
        name="router",
    )(h, rw_t)


def _topk_kernel(a_ref, tri_ref, idx_ref, g_ref, cs_scr, ga_scr, im_scr, gm_scr, *, seq, cap):
    aff = a_ref[...]
    ne = aff.shape[0]
    bits = pltpu.bitcast(aff, I32)
    tok = lax.broadcasted_iota(I32, aff.shape, 1)

    def count(mask):
        return jnp.sum(jnp.where(mask, 1.0, 0.0), axis=1, keepdims=True)

    def thr_step(it, thr):
        cand = thr | jnp.left_shift(jnp.int32(1), 30 - it)
        return jnp.where(count(bits >= cand) >= cap, cand, thr)

    thr = lax.fori_loop(0, 31, thr_step, jnp.zeros((ne, 1), I32))
    gt = bits > thr
    tie = bits == thr
    need = cap - count(gt)

    def tie_step(it, bound):
        cand = bound | jnp.left_shift(jnp.int32(1), 11 - it)
        return jnp.where(count(tie & (tok < cand)) < need, cand, bound)

    bound = lax.fori_loop(0, 12, tie_step, jnp.zeros((ne, 1), I32))
    sel = gt | (tie & (tok <= bound))
    sel_f = jnp.where(sel, 1.0, 0.0)

    nblk = seq // LANE
    stacked = jnp.concatenate([sel_f[:, j * LANE:(j + 1) * LANE] for j in range(nblk)], axis=0)
    pref = jnp.dot(stacked.astype(BF16), tri_ref[...], preferred_element_type=F32)
    off = jnp.zeros((ne, 1), F32)
    for j in range(nblk):
        pj = pref[j * ne:(j + 1) * ne, :]
        cs_scr[j] = pj + off
        ga_scr[j] = jnp.where(sel[:, j * LANE:(j + 1) * LANE], aff[:, j * LANE:(j + 1) * LANE], 0.0)
        off = off + pj[:, LANE - 1:LANE]
    rc = LANE
    jcol = lax.broadcasted_iota(I32, (rc, LANE), 0).astype(F32)
    lane = lax.broadcasted_iota(I32, (rc, LANE), 1)
    im_scr[...] = jnp.zeros_like(im_scr)
    gm_scr[...] = jnp.zeros_like(gm_scr)

    def per_expert(e, carry):
        for c in range(cap // rc):
            jc = jcol + float(c * rc)
            cnt = jnp.zeros((rc, LANE), F32)
            gacc = jnp.zeros((rc, LANE), F32)
            for j in range(nblk):
                cs_row = cs_scr[j, pl.ds(e, 1), :]
                ga_row = ga_scr[j, pl.ds(e, 1), :]
                cnt = cnt + jnp.where(cs_row <= jc, 1.0, 0.0)
                gacc = gacc + jnp.where(cs_row == jc + 1.0, ga_row, 0.0)
            rows = slice(c * rc, (c + 1) * rc)
            im_scr[rows, :] = jnp.where(lane == e, jnp.sum(cnt, axis=1, keepdims=True), im_scr[rows, :])
            gm_scr[rows, :] = jnp.where(lane == e, jnp.sum(gacc, axis=1, keepdims=True), gm_scr[rows, :])
        return carry

    lax.fori_loop(0, ne, per_expert, 0)
    idx_ref[0] = im_scr[...].T[0:ne, :].astype(I32)
    g_ref[0] = gm_scr[...].T[0:ne, :]


def topk_pallas(aff_t, tri, nb, seq, cap):
    ne = aff_t.shape[0]
    return pl.pallas_call(
        functools.partial(_topk_kernel, seq=seq, cap=cap),
        grid=(nb,),
        in_specs=[pl.BlockSpec((ne, seq), lambda b: (0, b)), pl.BlockSpec((LANE, LANE), lambda b: (0, 0))],
        out_specs=[pl.BlockSpec((1, ne, cap), lambda b: (b, 0, 0))] * 2,
        out_shape=[jax.ShapeDtypeStruct((nb, ne, cap), I32), jax.ShapeDtypeStruct((nb, ne, cap), F32)],
        scratch_shapes=[pltpu.VMEM((seq // LANE, ne, LANE), F32), pltpu.VMEM((seq // LANE, ne, LANE), F32),
                        pltpu.VMEM((cap, LANE), F32), pltpu.VMEM((cap, LANE), F32)],
        compiler_params=_cparams(("parallel",), VMEM_LIMIT),
        name="topk_select",
    )(aff_t, tri)


def _moe_kernel(idx_ref, gv_ref, hp_ref, wg_ref, wu_ref, wd_ref, o_ref, xs_ref, y_ref, *, cap):
    e = pl.program_id(1)

    @pl.when(e == 0)
    def _():
        o_ref[...] = jnp.zeros_like(o_ref)

    def gather(j, c):
        xs_ref[pl.ds(j, 1), :] = hp_ref[0, pl.ds(idx_ref[0, e, j], 1), :]
        return c

    lax.fori_loop(0, cap, gather, 0, unroll=8)
    xp = xs_ref[...]
    x = jnp.concatenate([pltpu.bitcast(xp & jnp.uint32(0xFFFF0000), F32),
                         pltpu.bitcast(xp << 16, F32)], axis=1).astype(BF16)
    hg = jnp.dot(x, wg_ref[0], preferred_element_type=F32)
    hu = jnp.dot(x, wu_ref[0], preferred_element_type=F32)
    hid = (hg * jax.nn.sigmoid(hg) * hu).astype(BF16)
    y_ref[...] = jnp.dot(hid, wd_ref[0], preferred_element_type=F32)

    def scatter(j, c):
        t = idx_ref[0, e, j]
        o_ref[0, pl.ds(t, 1), :] = o_ref[0, pl.ds(t, 1), :] + y_ref[pl.ds(j, 1), :] * gv_ref[0, e, j]
        return c

    lax.fori_loop(0, cap, scatter, 0, unroll=8)


def moe_pallas(idx, gates, hp, wg, wu, wd, nb, seq, cap):
    ne, d, ff = wg.shape
    smem = functools.partial(pl.BlockSpec, memory_space=pltpu.SMEM)
    return pl.pallas_call(
        functools.partial(_moe_kernel, cap=cap),
        grid=(nb, ne),
        in_specs=[
            smem((1, ne, cap), lambda b, e: (b, 0, 0)),
            smem((1, ne, cap), lambda b, e: (b, 0, 0)),
            pl.BlockSpec((1, seq, d // 2), lambda b, e: (b, 0, 0), pipeline_mode=pl.Buffered(1)),
            pl.BlockSpec((1, d, ff), lambda b, e: (e, 0, 0)),
            pl.BlockSpec((1, d, ff), lambda b, e: (e, 0, 0)),
            pl.BlockSpec((1, ff, d), lambda b, e: (e, 0, 0)),
        ],
        out_specs=pl.BlockSpec((1, seq, d), lambda b, e: (b, 0, 0), pipeline_mode=pl.Buffered(1)),
        out_shape=jax.ShapeDtypeStruct((nb, seq, d), F32),
        scratch_shapes=[pltpu.VMEM((cap, d // 2), U32), pltpu.VMEM((cap, d), F32)],
        compiler_params=_cparams(("parallel", "arbitrary"), VMEM_LIMIT),
        name="moe_experts",
    )(idx, gates, hp, wg, wu, wd)


def _rope_tables(seq):
    n_rows = seq // GRID_W
    rows = jnp.repeat(jnp.arange(n_rows, dtype=F32), GRID_W)
    cols = (jnp.arange(seq) % GRID_W).astype(F32)
    half = HEAD_DIM // 2
    inv = ROPE_THETA ** (-jnp.arange(0, half, 2, dtype=F32) / half)
    ar, ac = rows[:, None] * inv, cols[:, None] * inv
    cos = jnp.concatenate([jnp.cos(ar), jnp.cos(ar), jnp.cos(ac), jnp.cos(ac)], axis=1)
    sin = jnp.concatenate([-jnp.sin(ar), jnp.sin(ar), -jnp.sin(ac), jnp.sin(ac)], axis=1)
    return jnp.tile(cos, (1, 2)), jnp.tile(sin, (1, 2))


def kernel(x, ln_in_g, ln_in_b, w_in, hy_conv_w, hy_conv_b, hy_w1, hy_b1, hy_w2, hy_b2, hy_w3, hy_freq, hy_bias,
           s5_lam_re, s5_lam_im, s5_log_dt, s5_b_re, s5_b_im, s5_c_re, s5_c_im, s5_d, s5_glu_w, s5_glu_b,
           gqa_q_norm, gqa_k_norm, w_branch, w_out, ln1_g, ln1_b,
           router_w, exp_w_gate, exp_w_up, exp_w_down, ln2_g, ln2_b):
    nb, seq, d = x.shape
    n = nb * seq
    cap = (EC_CAPACITY * seq) // N_EXPERTS
    assert cap == TOPK_JB

    cos_t, sin_t = _rope_tables(seq)
    slopes = jnp.asarray(2.0 ** (-8.0 * np.arange(1, DIL_N_HEADS + 1) / DIL_N_HEADS), F32)
    ones_bd = jnp.asarray(np.kron(np.eye(LANE // HEAD_DIM), np.ones((HEAD_DIM, HEAD_DIM))), BF16)
    tri = jnp.asarray(np.triu(np.ones((LANE, LANE))), BF16)

    h = ln_pallas(x.reshape(n, d), ln_in_g, ln_in_b)
    for l in range(DEPTH):
        w_mix = w_in[l][:, :N_MIX].astype(BF16)
        w_gate = w_in[l][:, N_MIX:].astype(BF16)
        proj = inproj_pallas(h, w_mix)

        kext = _hyena_filters_ext(seq, hy_w1[l], hy_b1[l], hy_w2[l], hy_b2[l], hy_w3[l], hy_freq[l])
        u_t = proj[:, :COL_DIL].reshape(nb, seq, COL_DIL).transpose(2, 0, 1)
        y_a = hyena_pallas(u_t, kext, hy_conv_w[l], hy_conv_b[l], hy_bias[l])
        y_a = y_a.transpose(1, 2, 0).reshape(n, HY_WIDTH)

        proj3 = proj.reshape(nb, seq, N_MIX)
        y_b = dilated_pallas(proj3, slopes).reshape(n, 2 * LANE)

        bd, cd, a_b = _s5_operators(s5_lam_re[l], s5_lam_im[l], s5_log_dt[l],
                                    s5_b_re[l], s5_b_im[l], s5_c_re[l], s5_c_im[l], nb)
        y_sf, y_sb = s5_pallas(proj3, bd, cd, a_b)

        qn = jnp.tile(gqa_q_norm[l], LANE // HEAD_DIM).reshape(1, LANE)
        kn = jnp.tile(gqa_k_norm[l], LANE // HEAD_DIM).reshape(1, LANE)
        qr, kd, vd = gqa_prep_pallas(proj, cos_t, sin_t, qn, kn, ones_bd, seq)
        y_d = gqa_pallas(qr, kd, vd, nb, seq).reshape(n, GQA_WIDTH)

        h, hp = merge_pallas(h, y_a, y_b, y_sf.reshape(n, S5_WIDTH), y_sb.reshape(n, S5_WIDTH), proj, y_d,
                             w_gate, w_branch[l].astype(BF16), w_out[l].astype(BF16), s5_glu_w[l].astype(BF16), s5_glu_b[l], s5_d[l],
                             ln1_g[l], ln1_b[l])

        aff_t = router_pallas(h, router_w[l].T)
        idx, gates = topk_pallas(aff_t, tri, nb, seq, cap)
        ffn = moe_pallas(idx, gates, hp.reshape(nb, seq, d // 2), exp_w_gate[l].astype(BF16), exp_w_up[l].astype(BF16),
                         exp_w_down[l].astype(BF16), nb, seq, cap)
        h = res_ln_pallas(h, ffn.reshape(n, d), ln2_g[l], ln2_b[l])
    return h.reshape(nb, seq, d)
```

```python
import functools
import math

import jax
import jax.numpy as jnp
import numpy as np
from jax import lax
from jax.experimental import pallas as pl
from jax.experimental.pallas import tpu as pltpu

F32 = jnp.float32
BF16 = jnp.bfloat16
I32 = jnp.int32
U32 = jnp.uint32

D_MODEL = 1024
DEPTH = 4
HEAD_DIM = 64
GRID_W = 64
N_BRANCHES = 4
LN_EPS = 1e-5
RMS_EPS = 1e-6

HY_WIDTH = 256
HY_ORDER = 2
HY_POS_EMB = 33
HY_TARGET = 1e-2
HY_SHORT_DECAY_PCT = 0.3
HY_LONG_DECAY_PCT = 1.5

DIL_GROUPS = ((128, 1), (512, 4), (2048, 16))
DIL_HEADS_PER_GROUP = 4
DIL_N_HEADS = 12
DIL_WIDTH = DIL_N_HEADS * HEAD_DIM

S5_WIDTH = 256
S5_GROUP = 16
S5_N_GROUPS = 16
S5_STATE = 64

GQA_HEADS = 4
GQA_KV_HEADS = 2
GQA_WIDTH = GQA_HEADS * HEAD_DIM
GQA_KV_WIDTH = GQA_KV_HEADS * HEAD_DIM
ROPE_THETA = 10000.0

N_EXPERTS = 16
EC_CAPACITY = 2
EXPERT_FF = 1024

DN_ALPHA = (2 * DEPTH) ** 0.25

COL_HY = 0
COL_DIL = 3 * HY_WIDTH
COL_S5 = COL_DIL + 3 * DIL_WIDTH
COL_GQ = COL_S5 + S5_WIDTH
COL_GK = COL_GQ + GQA_WIDTH
COL_GV = COL_GK + GQA_KV_WIDTH
N_MIX = COL_GV + GQA_KV_WIDTH
N_IN = N_MIX + N_BRANCHES * D_MODEL

LANE = 128
SUBLANE = 8
VMEM_LIMIT = 56 * 1024 * 1024

HY_CB = 8
HY_BLK = 256
TOPK_JB = 512


def _cparams(sem, vmem=None):
    return pltpu.CompilerParams(dimension_semantics=sem, vmem_limit_bytes=vmem)


def _layer_norm(x, g, b):
    mu = jnp.mean(x, axis=-1, keepdims=True)
    xc = x - mu
    var = jnp.mean(xc * xc, axis=-1, keepdims=True)
    return xc * lax.rsqrt(var + LN_EPS) * g + b


def _ln_kernel(x_ref, g_ref, b_ref, o_ref):
    o_ref[...] = _layer_norm(x_ref[...], g_ref[...], b_ref[...])


def ln_pallas(x, g, b, tm=1024):
    n, d = x.shape
    return pl.pallas_call(
        _ln_kernel,
        grid=(n // tm,),
        in_specs=[pl.BlockSpec((tm, d), lambda i: (i, 0)),
                  pl.BlockSpec((1, d), lambda i: (0, 0)),
                  pl.BlockSpec((1, d), lambda i: (0, 0))],
        out_specs=pl.BlockSpec((tm, d), lambda i: (i, 0)),
        out_shape=jax.ShapeDtypeStruct((n, d), F32),
        compiler_params=_cparams(("parallel",)),
        name="ln_in",
    )(x, g.reshape(1, d), b.reshape(1, d))


def _res_ln_kernel(h_ref, f_ref, g_ref, b_ref, o_ref):
    o_ref[...] = _layer_norm(DN_ALPHA * h_ref[...] + f_ref[...], g_ref[...], b_ref[...])


def res_ln_pallas(h, f, g, b, tm=1024):
    n, d = h.shape
    return pl.pallas_call(
        _res_ln_kernel,
        grid=(n // tm,),
        in_specs=[pl.BlockSpec((tm, d), lambda i: (i, 0)),
                  pl.BlockSpec((tm, d), lambda i: (i, 0)),
                  pl.BlockSpec((1, d), lambda i: (0, 0)),
                  pl.BlockSpec((1, d), lambda i: (0, 0))],
        out_specs=pl.BlockSpec((tm, d), lambda i: (i, 0)),
        out_shape=jax.ShapeDtypeStruct((n, d), F32),
        compiler_params=_cparams(("parallel",)),
        name="res_ln",
    )(h, f, g.reshape(1, d), b.reshape(1, d))


def _inproj_kernel(h_ref, w_ref, o_ref):
    o_ref[...] = jnp.dot(h_ref[...].astype(BF16), w_ref[...], preferred_element_type=F32)


def inproj_pallas(h, w_bf16, tm=1024, tn=1280):
    n, d = h.shape
    nout = w_bf16.shape[1]
    return pl.pallas_call(
        _inproj_kernel,
        grid=(nout // tn, n // tm),
        in_specs=[pl.BlockSpec((tm, d), lambda j, i: (i, 0)),
                  pl.BlockSpec((d, tn), lambda j, i: (0, j))],
        out_specs=pl.BlockSpec((tm, tn), lambda j, i: (i, j)),
        out_shape=jax.ShapeDtypeStruct((n, nout), F32),
        compiler_params=_cparams(("parallel", "parallel"), VMEM_LIMIT),
        name="in_proj",
    )(h, w_bf16)


def _hy_kernel(cw_ref, cb_ref, hb_ref,
               v_ref, x1_ref, x2_ref, k_ref,
               o_ref,
               f2_ref, ua_ref, ub_ref,
               *, seq, nb):
    g = pl.program_id(0)
    nblk = seq // HY_BLK
    npad = nblk - 1
    nlb = 2 * seq // LANE
    rows = nb * nblk

    lane_i = lax.broadcasted_iota(I32, (LANE, LANE), 1)
    row_i = lax.broadcasted_iota(I32, (LANE, LANE), 0)
    keep_cur = lane_i >= row_i
    t_i = lax.broadcasted_iota(I32, (nb, seq), 1)

    def short_conv(x, ch):
        w0 = cw_ref[ch]
        w1 = cw_ref[3 * HY_WIDTH + ch]
        w2 = cw_ref[6 * HY_WIDTH + ch]
        xm = jnp.where(t_i == 0, 0.0, pltpu.roll(x, 1, 1))
        xp = jnp.where(t_i == seq - 1, 0.0, pltpu.roll(x, seq - 1, 1))
        return xm * w0 + x * w1 + xp * w2 + cb_ref[ch]

    def long_conv(u, order, ci):
        krow = k_ref[order, pl.ds(ci, 1), :]
        prev = None
        for n in range(nlb):
            blk = jnp.broadcast_to(krow[:, n * LANE:(n + 1) * LANE], (LANE, LANE))
            cur = pltpu.roll(blk, 0, 1, stride=1, stride_axis=0)
            if n >= 1:
                out = jnp.where(keep_cur, cur, prev).astype(BF16)
                f2_ref[0:LANE, n * LANE:(n + 1) * LANE] = out
                if n + 1 < nlb:
                    f2_ref[LANE:2 * LANE, (n + 1) * LANE:(n + 2) * LANE] = out
            prev = cur
        blocks = [u[:, j * HY_BLK:(j + 1) * HY_BLK] for j in range(nblk)]
        za = jnp.zeros((npad * nb, HY_BLK), F32)
        ua = jnp.concatenate([za] + blocks + [za, jnp.zeros((2 * nb, HY_BLK), F32)], axis=0)
        zb = jnp.zeros(((npad - 1) * nb, HY_BLK), F32)
        ub = jnp.concatenate([zb] + blocks + [za, jnp.zeros((3 * nb, HY_BLK), F32)], axis=0)
        ua_ref[...] = ua.astype(BF16)
        ub_ref[...] = ub.astype(BF16)
        acc = jnp.zeros((rows, HY_BLK), F32)
        for d in range(-npad, npad + 1):
            blk0 = npad - d
            if blk0 % 2 == 0:
                lhs = ua_ref[blk0 * nb:blk0 * nb + rows, :]
            else:
                lhs = ub_ref[(blk0 - 1) * nb:(blk0 - 1) * nb + rows, :]
            c0 = seq + HY_BLK * d
            acc = acc + jnp.dot(lhs, f2_ref[:, c0:c0 + HY_BLK], preferred_element_type=F32)
        return jnp.concatenate([acc[j * nb:(j + 1) * nb, :] for j in range(nblk)], axis=1)

    def body(ci, carry):
        ch = g * HY_CB + ci
        v = short_conv(v_ref[ci], ch)
        x1 = short_conv(x1_ref[ci], HY_WIDTH + ch)
        x2 = short_conv(x2_ref[ci], 2 * HY_WIDTH + ch)
        z = x1 * (long_conv(v, 0, ci) + v * hb_ref[ch])
        z = x2 * (long_conv(z, 1, ci) + z * hb_ref[HY_WIDTH + ch])
        o_ref[ci] = z
        return carry

    lax.fori_loop(0, HY_CB, body, 0)


def hyena_pallas(u_t, kext, conv_w, conv_b, hy_bias):
    _, nb, seq = u_t.shape
    assert nb == SUBLANE and seq % HY_BLK == 0
    nblk = seq // HY_BLK
    ncb = HY_WIDTH // HY_CB
    pad_rows = (2 * (nblk - 1) + nblk + 2) * nb
    grid_spec = pltpu.PrefetchScalarGridSpec(
        num_scalar_prefetch=3,
        grid=(ncb,),
        in_specs=[
            pl.BlockSpec((HY_CB, nb, seq), lambda g, *_: (g, 0, 0)),
            pl.BlockSpec((HY_CB, nb, seq), lambda g, *_: (ncb + g, 0, 0)),
            pl.BlockSpec((HY_CB, nb, seq), lambda g, *_: (2 * ncb + g, 0, 0)),
            pl.BlockSpec((HY_ORDER, HY_CB, 2 * seq), lambda g, *_: (0, g, 0)),
        ],
        out_specs=pl.BlockSpec((HY_CB, nb, seq), lambda g, *_: (g, 0, 0)),
        scratch_shapes=[
            pltpu.VMEM((HY_BLK, 2 * seq), BF16),
            pltpu.VMEM((pad_rows, HY_BLK), BF16),
            pltpu.VMEM((pad_rows, HY_BLK), BF16),
        ],
    )
    return pl.pallas_call(
        functools.partial(_hy_kernel, seq=seq, nb=nb),
        grid_spec=grid_spec,
        out_shape=jax.ShapeDtypeStruct((HY_WIDTH, nb, seq), F32),
        compiler_params=_cparams(("arbitrary",)),
        name="hyena_conv",
    )(conv_w.reshape(-1), conv_b, hy_bias.reshape(-1), u_t, u_t, u_t, kext)


def _hyena_filters_ext(seq, w1, b1, w2, b2, w3, freq):
    hp = lax.Precision.HIGHEST
    t = jnp.linspace(0.0, 1.0, seq, dtype=F32)[:, None]
    bands = (HY_POS_EMB - 1) // 2
    f = jnp.linspace(1e-4, bands - 1, bands, dtype=F32)[None, :]
    w = 2.0 * math.pi * jnp.arange(seq, dtype=F32)[:, None] / seq
    z = jnp.concatenate([t, jnp.cos(f * w), -jnp.sin(f * w)], axis=-1)
    h = jnp.sin(freq[0] * (jnp.dot(z, w1, precision=hp) + b1))
    h = jnp.sin(freq[1] * (jnp.dot(h, w2, precision=hp) + b2))
    h = jnp.dot(h, w3, precision=hp).reshape(seq, 2, HY_ORDER, HY_WIDTH)
    max_decay = math.log(HY_TARGET) / HY_SHORT_DECAY_PCT
    min_decay = math.log(HY_TARGET) / HY_LONG_DECAY_PCT
    deltas = jnp.abs(jnp.linspace(min_decay, max_decay, HY_WIDTH, dtype=F32))
    h = h * jnp.exp(-t * deltas)[:, None, None, :]
    fwd, bwd = h[:, 0], h[:, 1]
    k = jnp.concatenate([fwd, jnp.zeros_like(fwd[:1]), bwd[:0:-1]], axis=0)
    k = k * lax.rsqrt(jnp.sum(k * k, axis=0, keepdims=True))
    return jnp.roll(k, seq, axis=0).transpose(1, 2, 0)


DIL_QB = 128
DIL_KB = 256
DIL_HALF = 64


def _dil_kernel(slope_ref, q_ref, k_ref, v_ref, o_ref, oacc_ref, lse_ref, *, seq):
    pj = pl.program_id(1)
    g = pl.program_id(2)
    head1 = lax.broadcasted_iota(I32, (DIL_QB, LANE), 1) >= HEAD_DIM
    qi = lax.broadcasted_iota(I32, (DIL_QB, DIL_KB), 0)
    ki = lax.broadcasted_iota(I32, (DIL_QB, DIL_KB), 1)
    scale = HEAD_DIM ** -0.5

    for gi, (w, d) in enumerate(DIL_GROUPS):
        assert w // (2 * d) == DIL_HALF
        ln = seq // d

        @pl.when(g == gi)
        def _(gi=gi, d=d, ln=ln):
            def task(t, carry):
                r = t % d
                mq = (t // d) * DIL_QB
                mk = jnp.clip(mq - DIL_HALF, 0, ln - DIL_KB)
                q = q_ref[0, pl.ds(r + d * mq, DIL_QB, stride=d), :]
                k = k_ref[0, pl.ds(r + d * mk, DIL_KB, stride=d), :].astype(BF16)
                v = v_ref[0, pl.ds(r + d * mk, DIL_KB, stride=d), :].astype(BF16)
                rel = jnp.abs((ki + mk) - (qi + mq))
                valid = rel <= DIL_HALF
                dist = (rel * d).astype(F32)
                outs, lses = [], []
                for hh in range(2):
                    slope = slope_ref[4 * gi + 2 * pj + hh]
                    qm = jnp.where(head1 == (hh == 1), q, 0.0).astype(BF16)
                    s = lax.dot_general(qm, k, (((1,), (1,)), ((), ())),
                                        preferred_element_type=F32)
                    s = s * scale - slope * dist
                    s = jnp.where(valid, s, -jnp.inf)
                    m = jnp.max(s, axis=1, keepdims=True)
                    p = jnp.exp(s - m)
                    l = jnp.sum(p, axis=1, keepdims=True)
                    o = jnp.dot(p.astype(BF16), v, preferred_element_type=F32)
                    outs.append(o / l)
                    lses.append(m + jnp.log(l))
                o = jnp.where(head1, outs[1], outs[0])
                lse = jnp.where(head1, lses[1], lses[0])
                oacc_ref[gi, pl.ds(r + d * mq, DIL_QB, stride=d), :] = o
                lse_ref[gi, pl.ds(r + d * mq, DIL_QB, stride=d), :] = lse
                return carry

            lax.fori_loop(0, seq // DIL_QB, task, 0, unroll=4)

    @pl.when(g == len(DIL_GROUPS) - 1)
    def _():
        l0, l1, l2 = lse_ref[0], lse_ref[1], lse_ref[2]
        m = jnp.maximum(jnp.maximum(l0, l1), l2)
        e0, e1, e2 = jnp.exp(l0 - m), jnp.exp(l1 - m), jnp.exp(l2 - m)
        num = e0 * oacc_ref[0] + e1 * oacc_ref[1] + e2 * oacc_ref[2]
        o_ref[0] = num / (e0 + e1 + e2)


def dilated_pallas(proj3, slopes):
    nb, seq, _ = proj3.shape
    ng = len(DIL_GROUPS)
    qb, kb, vb = COL_DIL // LANE, (COL_DIL + DIL_WIDTH) // LANE, (COL_DIL + 2 * DIL_WIDTH) // LANE
    grid_spec = pltpu.PrefetchScalarGridSpec(
        num_scalar_prefetch=1,
        grid=(nb, 2, ng),
        in_specs=[
            pl.BlockSpec((1, seq, LANE), lambda b, p, g, *_: (b, 0, qb + 2 * g + p)),
            pl.BlockSpec((1, seq, LANE), lambda b, p, g, *_: (b, 0, kb + 2 * g + p)),
            pl.BlockSpec((1, seq, LANE), lambda b, p, g, *_: (b, 0, vb + 2 * g + p)),
        ],
        out_specs=pl.BlockSpec((1, seq, LANE), lambda b, p, g, *_: (b, 0, p)),
        scratch_shapes=[pltpu.VMEM((ng, seq, LANE), F32), pltpu.VMEM((ng, seq, LANE), F32)],
    )
    return pl.pallas_call(
        functools.partial(_dil_kernel, seq=seq),
        grid_spec=grid_spec,
        out_shape=jax.ShapeDtypeStruct((nb, seq, 2 * LANE), F32),
        compiler_params=_cparams(("parallel", "parallel", "arbitrary"), VMEM_LIMIT),
        name="dilated_attn",
    )(slopes, proj3, proj3, proj3)


S5_NS = S5_N_GROUPS * S5_STATE


def _s5_kernel(xf_ref, xb_ref, bd_ref, cd_ref, a_ref, yf_ref, yb_ref, buf_f, buf_b, st_ref, *, nb, tt):
    @pl.when(pl.program_id(0) == 0)
    def _():
        st_ref[...] = jnp.zeros_like(st_ref)

    width = xf_ref.shape[-1]
    nlb = S5_NS // LANE

    def project(x_ref, d, buf):
        x = jnp.concatenate([x_ref[:, t, :] for t in range(tt)], axis=0)
        bu = jnp.dot(x.astype(BF16), bd_ref[d], preferred_element_type=F32)
        for c in range(2 * nlb):
            buf[c] = bu[:, c * LANE:(c + 1) * LANE]

    project(xf_ref, 0, buf_f)
    project(xb_ref, 1, buf_b)

    def advance(buf, t, d, state):
        rows = pl.ds(pl.multiple_of(t * nb, nb), nb)
        out = []
        for c in range(nlb):
            lanes = slice(c * LANE, (c + 1) * LANE)
            a_re = a_ref[2 * d, :, lanes]
            a_im = a_ref[2 * d + 1, :, lanes]
            x_re, x_im = state[2 * c], state[2 * c + 1]
            n_re = a_re * x_re - a_im * x_im + buf[c, rows, :]
            n_im = a_re * x_im + a_im * x_re + buf[nlb + c, rows, :]
            buf[c, rows, :] = n_re
            buf[nlb + c, rows, :] = n_im
            out += [n_re, n_im]
        return tuple(out)

    def step(t, carry):
        return (advance(buf_f, t, 0, carry[0]), advance(buf_b, tt - 1 - t, 1, carry[1]))

    def load_state(d):
        return tuple(st_ref[2 * d + (i % 2), :, (i // 2) * LANE:(i // 2 + 1) * LANE] for i in range(2 * nlb))

    carry = lax.fori_loop(0, tt, step, (load_state(0), load_state(1)))
    for d in range(2):
        for i in range(2 * nlb):
            st_ref[2 * d + (i % 2), :, (i // 2) * LANE:(i // 2 + 1) * LANE] = carry[d][i]

    def readout(buf, d, y_ref):
        xs = jnp.concatenate([buf[c] for c in range(2 * nlb)], axis=1).astype(BF16)
        y = jnp.dot(xs, cd_ref[d], preferred_element_type=F32)
        for t in range(tt):
            y_ref[:, t, :] = y[t * nb:(t + 1) * nb, :]

    readout(buf_f, 0, yf_ref)
    readout(buf_b, 1, yb_ref)


def s5_pallas(proj3, bd, cd, a_b, tt=64):
    nb, seq, _ = proj3.shape
    nt = seq // tt
    w = S5_WIDTH
    cb = COL_S5 // w
    sd = jax.ShapeDtypeStruct
    return pl.pallas_call(
        functools.partial(_s5_kernel, nb=nb, tt=tt),
        grid=(nt,),
        in_specs=[
            pl.BlockSpec((nb, tt, w), lambda k: (0, k, cb)),
            pl.BlockSpec((nb, tt, w), lambda k: (0, nt - 1 - k, cb)),
            pl.BlockSpec((2, w, 2 * S5_NS), lambda k: (0, 0, 0)),
            pl.BlockSpec((2, 2 * S5_NS, w), lambda k: (0, 0, 0)),
            pl.BlockSpec((4, nb, S5_NS), lambda k: (0, 0, 0)),
        ],
        out_specs=[pl.BlockSpec((nb, tt, w), lambda k: (0, k, 0)),
                   pl.BlockSpec((nb, tt, w), lambda k: (0, nt - 1 - k, 0))],
        out_shape=[sd((nb, seq, w), F32), sd((nb, seq, w), F32)],
        scratch_shapes=[pltpu.VMEM((2 * S5_NS // LANE, nb * tt, LANE), F32),
                        pltpu.VMEM((2 * S5_NS // LANE, nb * tt, LANE), F32),
                        pltpu.VMEM((4, nb, S5_NS), F32)],
        compiler_params=_cparams(("arbitrary",), VMEM_LIMIT),
        name="s5_scan",
    )(proj3, proj3, bd, cd, a_b)


def _s5_operators(lam_re, lam_im, log_dt, b_re, b_im, c_re, c_im, nb):
    dt = jnp.exp(log_dt)[..., None]
    zr, zi = lam_re * dt, lam_im * dt
    mag = jnp.exp(zr)
    ar, ai = mag * jnp.cos(zi), mag * jnp.sin(zi)
    den = lam_re * lam_re + lam_im * lam_im
    qr = ((ar - 1.0) * lam_re + ai * lam_im) / den
    qi = (ai * lam_re - (ar - 1.0) * lam_im) / den
    bbr = qr[..., None] * b_re - qi[..., None] * b_im
    bbi = qr[..., None] * b_im + qi[..., None] * b_re
    eye = jnp.eye(S5_N_GROUPS, dtype=F32)
    bd_r = jnp.einsum('dgpc,gh->dgchp', bbr, eye).reshape(2, S5_WIDTH, S5_NS)
    bd_i = jnp.einsum('dgpc,gh->dgchp', bbi, eye).reshape(2, S5_WIDTH, S5_NS)
    bd = jnp.concatenate([bd_r, bd_i], axis=2)
    cd_r = jnp.einsum('dgcp,gh->dgphc', c_re, eye).reshape(2, S5_NS, S5_WIDTH)
    cd_i = jnp.einsum('dgcp,gh->dgphc', c_im, eye).reshape(2, S5_NS, S5_WIDTH)
    cd = jnp.concatenate([cd_r, -cd_i], axis=1)
    a_b = jnp.stack([ar[0], ai[0], ar[1], ai[1]], axis=0).reshape(4, 1, S5_NS)
    return bd.astype(BF16), cd.astype(BF16), jnp.broadcast_to(a_b, (4, nb, S5_NS))


def _gqa_prep_kernel(q_ref, k_ref, v_ref, cos_ref, sin_ref, qn_ref, kn_ref, ones_ref, qo_ref, ko_ref, vo_ref):
    lane = lax.broadcasted_iota(I32, cos_ref.shape, 1)
    first16 = (lane & 31) < 16
    cos = cos_ref[...]
    sin = sin_ref[...]
    ones = ones_ref[...]

    def rms(x, gain):
        sq = x * x
        hi = sq.astype(BF16)
        lo = (sq - hi.astype(F32)).astype(BF16)
        ms = (jnp.dot(hi, ones, preferred_element_type=F32)
              + jnp.dot(lo, ones, preferred_element_type=F32)) * (1.0 / HEAD_DIM)
        return x * lax.rsqrt(ms + RMS_EPS) * gain

    def rope(x):
        swapped = jnp.where(first16, pltpu.roll(x, LANE - 16, 1), pltpu.roll(x, 16, 1))
        return x * cos + swapped * sin

    scale = HEAD_DIM ** -0.5 * math.log2(math.e)
    q = q_ref[...]
    qn = qn_ref[...]
    for g in range(GQA_KV_HEADS):
        qg = rope(rms(q[:, g * LANE:(g + 1) * LANE], qn)) * scale
        sw = pltpu.roll(qg, HEAD_DIM, 1)
        keep = (lane >= g * HEAD_DIM) & (lane < (g + 1) * HEAD_DIM)
        for r in range(2):
            src = qg if r == g else sw
            qo_ref[0, 2 * g + r] = jnp.where(keep, src, 0.0).T.astype(BF16)
    ko_ref[...] = rope(rms(k_ref[...], kn_ref[...])).astype(BF16)
    vo_ref[0] = v_ref[...].T.astype(BF16)


def gqa_prep_pallas(proj, cos_t, sin_t, qn, kn, ones_bd, seq, tm=1024):
    n = proj.shape[0]
    nsb = seq // tm
    nb = n // seq
    w = GQA_WIDTH
    sd = jax.ShapeDtypeStruct
    return pl.pallas_call(
        _gqa_prep_kernel,
        grid=(n // tm,),
        in_specs=[
            pl.BlockSpec((tm, w), lambda i: (i, COL_GQ // w)),
            pl.BlockSpec((tm, LANE), lambda i: (i, COL_GK // LANE)),
            pl.BlockSpec((tm, LANE), lambda i: (i, COL_GV // LANE)),
            pl.BlockSpec((tm, LANE), lambda i: (i % nsb, 0)),
            pl.BlockSpec((tm, LANE), lambda i: (i % nsb, 0)),
            pl.BlockSpec((1, LANE), lambda i: (0, 0)),
            pl.BlockSpec((1, LANE), lambda i: (0, 0)),
            pl.BlockSpec((LANE, LANE), lambda i: (0, 0)),
        ],
        out_specs=[pl.BlockSpec((1, GQA_HEADS, LANE, tm), lambda i: (i // nsb, 0, 0, i % nsb)),
                   pl.BlockSpec((tm, LANE), lambda i: (i, 0)),
                   pl.BlockSpec((1, LANE, tm), lambda i: (i // nsb, 0, i % nsb))],
        out_shape=[sd((nb, GQA_HEADS, LANE, seq), BF16), sd((n, LANE), BF16), sd((nb, LANE, seq), BF16)],
        compiler_params=_cparams(("parallel",)),
        name="gqa_prep",
    )(proj, proj, proj, cos_t, sin_t, qn, kn, ones_bd)


GQA_TQ = 256
GQA_KC = 1024
GQA_AHEAD = 4


def _col_reduce(x, op):
    rows, cols = x.shape
    x = op(x.reshape(rows // (8 * SUBLANE), 8, SUBLANE, cols), axis=1)
    x = op(x, axis=0)
    return op(x, axis=0, keepdims=True)


def _gqa_kernel(q_ref, k_ref, v_ref, o_ref, *, seq):
    nchunk = seq // GQA_KC
    jobs = [(r, c) for c in range(nchunk) for r in range(2)]

    def scores(job):
        r, c = job
        k = k_ref[0, c * GQA_KC:(c + 1) * GQA_KC, :]
        return jnp.dot(k, q_ref[0, r], preferred_element_type=F32)

    m = [jnp.full((1, GQA_TQ), -jnp.inf, F32)] * 2
    l = [jnp.zeros((1, GQA_TQ), F32)] * 2
    acc = [jnp.zeros((HEAD_DIM, GQA_TQ), F32)] * 2
    pend = [scores(job) for job in jobs[:GQA_AHEAD]]
    for i, (r, c) in enumerate(jobs):
        s = pend.pop(0)
        if i + GQA_AHEAD < len(jobs):
            pend.append(scores(jobs[i + GQA_AHEAD]))
        v_t = v_ref[0, :, c * GQA_KC:(c + 1) * GQA_KC]
        mn = jnp.maximum(m[r], _col_reduce(s, jnp.max))
        alpha = jnp.exp2(m[r] - mn)
        p = jnp.exp2(s - mn)
        l[r] = alpha * l[r] + _col_reduce(p, jnp.sum)
        acc[r] = alpha * acc[r] + jnp.dot(v_t, p.astype(BF16), preferred_element_type=F32)
        m[r] = mn
    outs = [acc[r] / l[r] for r in range(2)]
    o_ref[0] = jnp.concatenate(outs, axis=0).T


def gqa_pallas(q_t, kr, v_t, nb, seq):
    k3 = kr.reshape(nb, seq, LANE)
    return pl.pallas_call(
        functools.partial(_gqa_kernel, seq=seq),
        grid=(nb, GQA_KV_HEADS, seq // GQA_TQ),
        in_specs=[
            pl.BlockSpec((1, 2, LANE, GQA_TQ), lambda b, g, i: (b, g, 0, i)),
            pl.BlockSpec((1, seq, LANE), lambda b, g, i: (b, 0, 0)),
            pl.BlockSpec((1, HEAD_DIM, seq), lambda b, g, i: (b, g, 0)),
        ],
        out_specs=pl.BlockSpec((1, GQA_TQ, LANE), lambda b, g, i: (b, i, g)),
        out_shape=jax.ShapeDtypeStruct((nb, seq, GQA_WIDTH), F32),
        compiler_params=_cparams(("parallel", "parallel", "arbitrary"), VMEM_LIMIT),
        name="gqa_flash",
    )(q_t, k3, v_t)


def _merge_kernel(h_ref, ya_ref, yb_ref, ysf_ref, ysb_ref, us_ref, yd_ref,
                  wg_ref, wb_ref, wo_ref, glw_ref, glb_ref, s5d_ref, g_ref, b_ref,
                  o_ref, op_ref):
    h = h_ref[...]
    hb = h.astype(BF16)
    y = ysf_ref[...] + ysb_ref[...] + s5d_ref[...] * us_ref[...]
    c0 = math.sqrt(2.0 / math.pi)
    z = 0.5 * y * (1.0 + jnp.tanh(c0 * (y + 0.044715 * (y * y * y))))
    zl = jnp.dot(z.astype(BF16), glw_ref[...], preferred_element_type=F32) + glb_ref[...]
    yc = z * jax.nn.sigmoid(zl)
    merged = None
    for i, yv in enumerate((ya_ref[...], yb_ref[...], yc, yd_ref[...])):
        gate = jax.nn.sigmoid(jnp.dot(hb, wg_ref[:, i * D_MODEL:(i + 1) * D_MODEL],
                                      preferred_element_type=F32))
        term = gate * jnp.dot(yv.astype(BF16), wb_ref[i], preferred_element_type=F32)
        merged = term if merged is None else merged + term
    mix = jnp.dot(merged.astype(BF16), wo_ref[...], preferred_element_type=F32)
    out = _layer_norm(DN_ALPHA * h + mix, g_ref[...], b_ref[...])
    o_ref[...] = out
    half = D_MODEL // 2
    hi = pltpu.bitcast(out[:, :half].astype(BF16).astype(F32), U32)
    lo = pltpu.bitcast(out[:, half:].astype(BF16).astype(F32), U32)
    op_ref[...] = hi | (lo >> 16)


def merge_pallas(h, ya, yb, ysf, ysb, proj, yd, wg, wb, wo, glw, glb, s5d, g, b, tm=256):
    n, d = h.shape
    bw = HY_WIDTH
    row = lambda i: (i, 0)
    fix2 = lambda i: (0, 0)
    return pl.pallas_call(
        _merge_kernel,
        grid=(n // tm,),
        in_specs=[
            pl.BlockSpec((tm, d), row),
            pl.BlockSpec((tm, bw), row),
            pl.BlockSpec((tm, bw), row),
            pl.BlockSpec((tm, bw), row),
            pl.BlockSpec((tm, bw), row),
            pl.BlockSpec((tm, bw), lambda i: (i, COL_S5 // bw)),
            pl.BlockSpec((tm, bw), row),
            pl.BlockSpec((d, N_BRANCHES * d), fix2),
            pl.BlockSpec((N_BRANCHES, bw, d), lambda i: (0, 0, 0)),
            pl.BlockSpec((d, d), fix2),
            pl.BlockSpec((bw, bw), fix2),
            pl.BlockSpec((1, bw), fix2),
            pl.BlockSpec((1, bw), fix2),
            pl.BlockSpec((1, d), fix2),
            pl.BlockSpec((1, d), fix2),
        ],
        out_specs=[pl.BlockSpec((tm, d), row), pl.BlockSpec((tm, d // 2), row)],
        out_shape=[jax.ShapeDtypeStruct((n, d), F32), jax.ShapeDtypeStruct((n, d // 2), U32)],
        compiler_params=_cparams(("parallel",), VMEM_LIMIT),
        name="gated_merge",
    )(h, ya, yb, ysf, ysb, proj, yd, wg, wb, wo, glw, glb.reshape(1, bw), s5d.reshape(1, bw),
      g.reshape(1, d), b.reshape(1, d))


def _router_kernel(h_ref, w_ref, o_ref):
    h = h_ref[...]
    w = w_ref[...]
    h_hi = h.astype(BF16)
    h_lo = (h - h_hi.astype(F32)).astype(BF16)
    w_hi = w.astype(BF16)
    w_lo = (w - w_hi.astype(F32)).astype(BF16)
    nt = (((1,), (1,)), ((), ()))
    logits = (lax.dot_general(w_hi, h_hi, nt, preferred_element_type=F32)
              + lax.dot_general(w_hi, h_lo, nt, preferred_element_type=F32)
              + lax.dot_general(w_lo, h_hi, nt, preferred_element_type=F32))
    m = jnp.max(logits, axis=0, keepdims=True)
    e = jnp.exp(logits - m)
    o_ref[...] = e / jnp.sum(e, axis=0, keepdims=True)


def router_pallas(h, rw_t, tm=1024):
    n, d = h.shape
    ne = rw_t.shape[0]
    return pl.pallas_call(
        _router_kernel,
        grid=(n // tm,),
        in_specs=[pl.BlockSpec((tm, d), lambda i: (i, 0)), pl.BlockSpec((ne, d), lambda i: (0, 0))],
        out_specs=pl.BlockSpec((ne, tm), lambda i: (0, i)),
        out_shape=jax.ShapeDtypeStruct((ne, n), F32),
        compiler_params=_cparams(("parallel",)),
        name="router",
    )(h, rw_t)


def _topk_kernel(a_ref, tri_ref, idx_ref, g_ref, cs_scr, ga_scr, im_scr, gm_scr, *, seq, cap):
    aff = a_ref[...]
    ne = aff.shape[0]
    bits = pltpu.bitcast(aff, I32)
    tok = lax.broadcasted_iota(I32, aff.shape, 1)

    def count(mask):
        return jnp.sum(jnp.where(mask, 1.0, 0.0), axis=1, keepdims=True)

    def thr_step(it, thr):
        cand = thr | jnp.left_shift(jnp.int32(1), 30 - it)
        return jnp.where(count(bits >= cand) >= cap, cand, thr)

    thr = lax.fori_loop(0, 31, thr_step, jnp.zeros((ne, 1), I32))
    gt = bits > thr
    tie = bits == thr
    need = cap - count(gt)

    def tie_step(it, bound):
        cand = bound | jnp.left_shift(jnp.int32(1), 11 - it)
        return jnp.where(count(tie & (tok < cand)) < need, cand, bound)

    bound = lax.fori_loop(0, 12, tie_step, jnp.zeros((ne, 1), I32))
    sel = gt | (tie & (tok <= bound))
    sel_f = jnp.where(sel, 1.0, 0.0)

    nblk = seq // LANE
    stacked = jnp.concatenate([sel_f[:, j * LANE:(j + 1) * LANE] for j in range(nblk)], axis=0)
    pref = jnp.dot(stacked.astype(BF16), tri_ref[...], preferred_element_type=F32)
    off = jnp.zeros((ne, 1), F32)
    for j in range(nblk):
        pj = pref[j * ne:(j + 1) * ne, :]
        cs_scr[j] = pj + off
        ga_scr[j] = jnp.where(sel[:, j * LANE:(j + 1) * LANE], aff[:, j * LANE:(j + 1) * LANE], 0.0)
        off = off + pj[:, LANE - 1:LANE]
    rc = LANE
    jcol = lax.broadcasted_iota(I32, (rc, LANE), 0).astype(F32)
    lane = lax.broadcasted_iota(I32, (rc, LANE), 1)
    im_scr[...] = jnp.zeros_like(im_scr)
    gm_scr[...] = jnp.zeros_like(gm_scr)

    def per_expert(e, carry):
        for c in range(cap // rc):
            jc = jcol + float(c * rc)
            cnt = jnp.zeros((rc, LANE), F32)
            gacc = jnp.zeros((rc, LANE), F32)
            for j in range(nblk):
                cs_row = cs_scr[j, pl.ds(e, 1), :]
                ga_row = ga_scr[j, pl.ds(e, 1), :]
                cnt = cnt + jnp.where(cs_row <= jc, 1.0, 0.0)
                gacc = gacc + jnp.where(cs_row == jc + 1.0, ga_row, 0.0)
            rows = slice(c * rc, (c + 1) * rc)
            im_scr[rows, :] = jnp.where(lane == e, jnp.sum(cnt, axis=1, keepdims=True), im_scr[rows, :])
            gm_scr[rows, :] = jnp.where(lane == e, jnp.sum(gacc, axis=1, keepdims=True), gm_scr[rows, :])
        return carry

    lax.fori_loop(0, ne, per_expert, 0)
    idx_ref[0] = im_scr[...].T[0:ne, :].astype(I32)
    g_ref[0] = gm_scr[...].T[0:ne, :]


def topk_pallas(aff_t, tri, nb, seq, cap):
    ne = aff_t.shape[0]
    return pl.pallas_call(
        functools.partial(_topk_kernel, seq=seq, cap=cap),
        grid=(nb,),
        in_specs=[pl.BlockSpec((ne, seq), lambda b: (0, b)), pl.BlockSpec((LANE, LANE), lambda b: (0, 0))],
        out_specs=[pl.BlockSpec((1, ne, cap), lambda b: (b, 0, 0))] * 2,
        out_shape=[jax.ShapeDtypeStruct((nb, ne, cap), I32), jax.ShapeDtypeStruct((nb, ne, cap), F32)],
        scratch_shapes=[pltpu.VMEM((seq // LANE, ne, LANE), F32), pltpu.VMEM((seq // LANE, ne, LANE), F32),
                        pltpu.VMEM((cap, LANE), F32), pltpu.VMEM((cap, LANE), F32)],
        compiler_params=_cparams(("parallel",), VMEM_LIMIT),
        name="topk_select",
    )(aff_t, tri)


MOE_SCATTER_GROUP = 8


def _moe_kernel(idx_ref, gv_ref, hp_ref, wg_ref, wu_ref, wd_ref, o_ref, xs_ref, y_ref, *, cap):
    e = pl.program_id(1)

    @pl.when(e == 0)
    def _():
        o_ref[...] = jnp.zeros_like(o_ref)

    nrow = cap // LANE
    for j in range(cap):
        xs_ref[j:j + 1, :] = hp_ref[0, pl.ds(idx_ref[0, e * nrow + j // LANE, j % LANE], 1), :]
    xp = xs_ref[...]
    x = jnp.concatenate([pltpu.bitcast(xp & jnp.uint32(0xFFFF0000), F32),
                         pltpu.bitcast(xp << 16, F32)], axis=1).astype(BF16)
    hg = jnp.dot(x, wg_ref[0], preferred_element_type=F32)
    hu = jnp.dot(x, wu_ref[0], preferred_element_type=F32)
    hid = (hg * jax.nn.sigmoid(hg) * hu).astype(BF16)
    y_ref[...] = jnp.dot(hid, wd_ref[0], preferred_element_type=F32)

    for j0 in range(0, cap, MOE_SCATTER_GROUP):
        js = range(j0, j0 + MOE_SCATTER_GROUP)
        toks = [idx_ref[0, e * nrow + j // LANE, j % LANE] for j in js]
        vals = [o_ref[0, pl.ds(t, 1), :] + y_ref[j:j + 1, :] * gv_ref[0, e * nrow + j // LANE, j % LANE]
                for j, t in zip(js, toks)]
        for t, val in zip(toks, vals):
            o_ref[0, pl.ds(t, 1), :] = val


def moe_pallas(idx, gates, hp, wg, wu, wd, nb, seq, cap):
    ne, d, ff = wg.shape
    smem = functools.partial(pl.BlockSpec, memory_space=pltpu.SMEM)
    srows = ne * cap // LANE
    idx = idx.reshape(nb, srows, LANE)
    gates = gates.reshape(nb, srows, LANE)
    return pl.pallas_call(
        functools.partial(_moe_kernel, cap=cap),
        grid=(nb, ne),
        in_specs=[
            smem((1, srows, LANE), lambda b, e: (b, 0, 0)),
            smem((1, srows, LANE), lambda b, e: (b, 0, 0)),
            pl.BlockSpec((1, seq, d // 2), lambda b, e: (b, 0, 0), pipeline_mode=pl.Buffered(1)),
            pl.BlockSpec((1, d, ff), lambda b, e: (e, 0, 0)),
            pl.BlockSpec((1, d, ff), lambda b, e: (e, 0, 0)),
            pl.BlockSpec((1, ff, d), lambda b, e: (e, 0, 0)),
        ],
        out_specs=pl.BlockSpec((1, seq, d), lambda b, e: (b, 0, 0), pipeline_mode=pl.Buffered(1)),
        out_shape=jax.ShapeDtypeStruct((nb, seq, d), F32),
        scratch_shapes=[pltpu.VMEM((cap, d // 2), U32), pltpu.VMEM((cap, d), F32)],
        compiler_params=_cparams(("parallel", "arbitrary"), VMEM_LIMIT),
        name="moe_experts",
    )(idx, gates, hp, wg, wu, wd)


def _rope_tables(seq):
    n_rows = seq // GRID_W
    rows = jnp.repeat(jnp.arange(n_rows, dtype=F32), GRID_W)
    cols = (jnp.arange(seq) % GRID_W).astype(F32)
    half = HEAD_DIM // 2
    inv = ROPE_THETA ** (-jnp.arange(0, half, 2, dtype=F32) / half)
    ar, ac = rows[:, None] * inv, cols[:, None] * inv
    cos = jnp.concatenate([jnp.cos(ar), jnp.cos(ar), jnp.cos(ac), jnp.cos(ac)], axis=1)
    sin = jnp.concatenate([-jnp.sin(ar), jnp.sin(ar), -jnp.sin(ac), jnp.sin(ac)], axis=1)
    return jnp.tile(cos, (1, 2)), jnp.tile(sin, (1, 2))


def kernel(x, ln_in_g, ln_in_b, w_in, hy_conv_w, hy_conv_b, hy_w1, hy_b1, hy_w2, hy_b2, hy_w3, hy_freq, hy_bias,
           s5_lam_re, s5_lam_im, s5_log_dt, s5_b_re, s5_b_im, s5_c_re, s5_c_im, s5_d, s5_glu_w, s5_glu_b,
           gqa_q_norm, gqa_k_norm, w_branch, w_out, ln1_g, ln1_b,
           router_w, exp_w_gate, exp_w_up, exp_w_down, ln2_g, ln2_b):
    nb, seq, d = x.shape
    n = nb * seq
    cap = (EC_CAPACITY * seq) // N_EXPERTS
    assert cap == TOPK_JB

    cos_t, sin_t = _rope_tables(seq)
    slopes = jnp.asarray(2.0 ** (-8.0 * np.arange(1, DIL_N_HEADS + 1) / DIL_N_HEADS), F32)
    ones_bd = jnp.asarray(np.kron(np.eye(LANE // HEAD_DIM), np.ones((HEAD_DIM, HEAD_DIM))), BF16)
    tri = jnp.asarray(np.triu(np.ones((LANE, LANE))), BF16)

    h = ln_pallas(x.reshape(n, d), ln_in_g, ln_in_b)
    for l in range(DEPTH):
        w_mix = w_in[l][:, :N_MIX].astype(BF16)
        w_gate = w_in[l][:, N_MIX:].astype(BF16)
        proj = inproj_pallas(h, w_mix)

        kext = _hyena_filters_ext(seq, hy_w1[l], hy_b1[l], hy_w2[l], hy_b2[l], hy_w3[l], hy_freq[l])
        u_t = proj[:, :COL_DIL].reshape(nb, seq, COL_DIL).transpose(2, 0, 1)
        y_a = hyena_pallas(u_t, kext, hy_conv_w[l], hy_conv_b[l], hy_bias[l])
        y_a = y_a.transpose(1, 2, 0).reshape(n, HY_WIDTH)

        proj3 = proj.reshape(nb, seq, N_MIX)
        y_b = dilated_pallas(proj3, slopes).reshape(n, 2 * LANE)

        bd, cd, a_b = _s5_operators(s5_lam_re[l], s5_lam_im[l], s5_log_dt[l],
                                    s5_b_re[l], s5_b_im[l], s5_c_re[l], s5_c_im[l], nb)
        y_sf, y_sb = s5_pallas(proj3, bd, cd, a_b)

        qn = jnp.tile(gqa_q_norm[l], LANE // HEAD_DIM).reshape(1, LANE)
        kn = jnp.tile(gqa_k_norm[l], LANE // HEAD_DIM).reshape(1, LANE)
        qr, kd, vd = gqa_prep_pallas(proj, cos_t, sin_t, qn, kn, ones_bd, seq)
        y_d = gqa_pallas(qr, kd, vd, nb, seq).reshape(n, GQA_WIDTH)

        h, hp = merge_pallas(h, y_a, y_b, y_sf.reshape(n, S5_WIDTH), y_sb.reshape(n, S5_WIDTH), proj, y_d,
                             w_gate, w_branch[l].astype(BF16), w_out[l].astype(BF16), s5_glu_w[l].astype(BF16), s5_glu_b[l], s5_d[l],
                             ln1_g[l], ln1_b[l])

        aff_t = router_pallas(h, router_w[l].T)
        idx, gates = topk_pallas(aff_t, tri, nb, seq, cap)
        ffn = moe_pallas(idx, gates, hp.reshape(nb, seq, d // 2), exp_w_gate[l].astype(BF16), exp_w_up[l].astype(BF16),
                         exp_w_down[l].astype(BF16), nb, seq, cap)
        h = res_ln_pallas(h, ffn.reshape(n, d), ln2_g[l], ln2_b[l])
    return h.reshape(nb, seq, d)
```
